```python
import math
import jax, jax.numpy as jnp
from jax import lax
import numpy as np

D_MODEL = 2048
BATCH = 4
SEQ = 4096
DEPTH = 1

D_MIX = D_MODEL
D_CONV = D_MIX // 2
D_RET = D_MIX - D_CONV
CONV_WIDTH = 31
RET_HEADS = 8
RET_HEAD_DIM = D_RET // RET_HEADS
RET_CHUNK = 128
ROPE_BASE = 10000.0
N_IN = 2 * D_CONV + 4 * D_RET
N_GROUPS = 4
EXPERTS_PER_GROUP = 8
N_EXPERTS = N_GROUPS * EXPERTS_PER_GROUP
TOP_K = 2
D_EXPERT = D_MODEL // 2
MOE_BLOCK = 128
LN_EPS = 1e-5
DN_ALPHA = (2 * DEPTH) ** 0.25
DN_BETA = (8 * DEPTH) ** -0.25

kernel_name = "hybrid_conformer_retention_hmoe_deepnorm_adaln"


def layer_norm(x, gain=None, bias=None):
    xf = x.astype(jnp.float32)
    mu = jnp.mean(xf, axis=-1, keepdims=True)
    var = jnp.mean(jnp.square(xf - mu), axis=-1, keepdims=True)
    y = (xf - mu) * lax.rsqrt(var + LN_EPS)
    if gain is not None:
        y = y * gain.astype(jnp.float32) + bias.astype(jnp.float32)
    return y.astype(x.dtype)


def rotary(x, positions):
    half = x.shape[-1] // 2
    inv_freq = jnp.exp(-math.log(ROPE_BASE) * jnp.arange(half, dtype=jnp.float32) / half)
    ang = positions.astype(jnp.float32)[..., None] * inv_freq
    cos = jnp.cos(ang)[:, :, None, :]
    sin = jnp.sin(ang)[:, :, None, :]
    x1, x2 = x[..., :half], x[..., half:]
    return jnp.concatenate([x1 * cos - x2 * sin, x1 * sin + x2 * cos], axis=-1)


def chunkwise_retention(q, k, v):
    B, T, H, d = q.shape
    C = RET_CHUNK
    N = T // C
    log_gamma = jnp.log1p(-jnp.exp2(-5.0 - jnp.arange(H, dtype=jnp.float32)))
    idx = jnp.arange(C, dtype=jnp.float32)
    diff = idx[:, None] - idx[None, :]
    causal = diff >= 0
    decay = jnp.where(causal[None], jnp.exp(jnp.where(causal, diff, 0.0)[None] * log_gamma[:, None, None]), 0.0)
    xi = jnp.exp((idx[None, :] + 1.0) * log_gamma[:, None])
    zeta = jnp.exp((C - 1.0 - idx[None, :]) * log_gamma[:, None])
    chunk_decay = jnp.exp(C * log_gamma)
    qc = q.reshape(B, N, C, H, d)
    kc = k.reshape(B, N, C, H, d)
    vc = v.reshape(B, N, C, H, d)
    scores = jnp.einsum('bnihd,bnjhd->bnhij', qc, kc) * decay[None, None]
    inner = jnp.einsum('bnhij,bnjhe->bnihe', scores, vc)
    kz = kc * zeta.T[None, None, :, :, None]
    kv = jnp.einsum('bnjhd,bnjhe->bnhde', kz, vc)

    def step(state, kv_n):
        return state * chunk_decay[None, :, None, None] + kv_n, state

    _, prev = lax.scan(step, jnp.zeros((B, H, d, d), jnp.float32), jnp.moveaxis(kv, 1, 0))
    prev = jnp.moveaxis(prev, 0, 1)
    qx = qc * xi.T[None, None, :, :, None]
    cross = jnp.einsum('bnihd,bnhde->bnihe', qx, prev)
    return (inner + cross).reshape(B, T, H, d)


def token_mixer(u, positions, w_in, conv_w, conv_b, conv_ln_g, conv_ln_b, w_out):
    B, T, _ = u.shape
    proj = u @ w_in
    a = proj[..., :D_CONV]
    b = proj[..., D_CONV:2 * D_CONV]
    o = 2 * D_CONV
    q = proj[..., o:o + D_RET]
    k = proj[..., o + D_RET:o + 2 * D_RET]
    v = proj[..., o + 2 * D_RET:o + 3 * D_RET]
    g = proj[..., o + 3 * D_RET:o + 4 * D_RET]
    h = a * jax.nn.sigmoid(b)
    h = lax.conv_general_dilated(h, conv_w[:, None, :], window_strides=(1,),
                                 padding=[(CONV_WIDTH - 1, 0)],
                                 dimension_numbers=('NWC', 'WIO', 'NWC'),
                                 feature_group_count=D_CONV) + conv_b
    h = jax.nn.silu(layer_norm(h, conv_ln_g, conv_ln_b))
    f32 = jnp.float32
    qh = rotary(q.reshape(B, T, RET_HEADS, RET_HEAD_DIM).astype(f32), positions)
    kh = rotary(k.reshape(B, T, RET_HEADS, RET_HEAD_DIM).astype(f32), positions) * (RET_HEAD_DIM ** -0.5)
    vh = v.reshape(B, T, RET_HEADS, RET_HEAD_DIM).astype(f32)
    r = layer_norm(chunkwise_retention(qh, kh, vh))
    r = jax.nn.silu(g) * r.reshape(B, T, D_RET).astype(u.dtype)
    return jnp.concatenate([h, r], axis=-1) @ w_out


def hierarchical_moe(u, w_group_router, b_group_router, w_expert_router, b_expert_router,
                     w_gate, w_up, w_down):
    B, T, D = u.shape
    n_tok = B * T
    xf = u.reshape(n_tok, D)
    g_logits = (xf @ w_group_router).astype(jnp.float32) + b_group_router.astype(jnp.float32)
    grp = jnp.argmax(g_logits, axis=-1)
    g_w = jnp.take_along_axis(jax.nn.softmax(g_logits, axis=-1), grp[:, None], axis=-1)[:, 0]
    e_all = (xf @ w_expert_router).astype(jnp.float32) + b_expert_router.astype(jnp.float32)
    e_logits = jnp.take_along_axis(e_all.reshape(n_tok, N_GROUPS, EXPERTS_PER_GROUP),
                                   grp[:, None, None], axis=1)[:, 0]
    top_logits, top_local = lax.top_k(e_logits, TOP_K)
    weights = g_w[:, None] * jax.nn.softmax(top_logits, axis=-1)
    expert_ids = grp[:, None] * EXPERTS_PER_GROUP + top_local
    n_slots = n_tok * TOP_K
    flat_e = expert_ids.reshape(-1)
    flat_tok = jnp.repeat(jnp.arange(n_tok, dtype=jnp.int32), TOP_K)
    flat_w = weights.reshape(-1)
    counts = jnp.bincount(flat_e, length=N_EXPERTS)
    padded = (counts + MOE_BLOCK - 1) // MOE_BLOCK * MOE_BLOCK
    pad_end = jnp.cumsum(padded)
    pad_start = pad_end - padded
    start = jnp.cumsum(counts) - counts
    order = jnp.argsort(flat_e)
    e_sorted = flat_e[order]
    dest = pad_start[e_sorted] + jnp.arange(n_slots) - start[e_sorted]
    cap = -(-n_slots // MOE_BLOCK) * MOE_BLOCK + N_EXPERTS * MOE_BLOCK
    slot_tok = jnp.zeros((cap,), jnp.int32).at[dest].set(flat_tok[order])
    slot_w = jnp.zeros((cap,), jnp.float32).at[dest].set(flat_w[order])
    n_blocks = cap // MOE_BLOCK
    block_e = jnp.minimum(jnp.searchsorted(pad_end, jnp.arange(n_blocks) * MOE_BLOCK, side='right'),
                          N_EXPERTS - 1)
    xb = xf[slot_tok].reshape(n_blocks, MOE_BLOCK, D)

    def expert_block(args):
        xb_i, e = args
        hid = jax.nn.silu(xb_i @ w_gate[e]) * (xb_i @ w_up[e])
        return hid @ w_down[e]

    yb = lax.map(expert_block, (xb, block_e)).reshape(cap, D)
    y = yb * slot_w[:, None].astype(yb.dtype)
    out = jnp.zeros((n_tok, D), yb.dtype).at[slot_tok].add(y)
    return out.reshape(B, T, D)


def setup_inputs(seed: int = 0) -> dict:
    key = jax.random.key(seed)
    ks = jax.random.split(key, 24)
    f32 = jnp.float32
    L, D = DEPTH, D_MODEL

    def nrm(k, shape, scale):
        return jax.random.normal(k, shape, f32) * scale

    x = nrm(ks[0], (BATCH, SEQ, D), 1.0)
    c = nrm(ks[1], (BATCH, D), 1.0)
    strides = jax.random.randint(ks[2], (BATCH, SEQ), 1, 3)
    positions = (jnp.cumsum(strides, axis=1) - strides[:, :1]).astype(jnp.int32)
    w_ada = nrm(ks[3], (L, D, 6 * D), 0.25 * D ** -0.5)
    b_ada = nrm(ks[4], (L, 6 * D), 0.01)
    col_scale = jnp.concatenate([jnp.ones((2 * D_CONV + 2 * D_RET,), f32),
                                 jnp.full((D_RET,), DN_BETA, f32),
                                 jnp.ones((D_RET,), f32)])
    w_in = nrm(ks[5], (L, D, N_IN), D ** -0.5) * col_scale
    conv_w = nrm(ks[6], (L, CONV_WIDTH, D_CONV), CONV_WIDTH ** -0.5)
    conv_b = nrm(ks[7], (L, D_CONV), 0.01)
    conv_ln_g = 1.0 + nrm(ks[8], (L, D_CONV), 0.01)
    conv_ln_b = nrm(ks[9], (L, D_CONV), 0.01)
    w_out = nrm(ks[10], (L, D_MIX, D), DN_BETA * D_MIX ** -0.5)
    ln1_g = 1.0 + nrm(ks[11], (L, D), 0.01)
    ln1_b = nrm(ks[12], (L, D), 0.01)
    w_group_router = nrm(ks[13], (L, D, N_GROUPS), D ** -0.5)
    b_group_router = nrm(ks[14], (L, N_GROUPS), 0.01)
    w_expert_router = nrm(ks[15], (L, D, N_EXPERTS), D ** -0.5)
    b_expert_router = nrm(ks[16], (L, N_EXPERTS), 0.01)
    w_gate = nrm(ks[17], (L, N_EXPERTS, D, D_EXPERT), D ** -0.5)
    w_up = nrm(ks[18], (L, N_EXPERTS, D, D_EXPERT), D ** -0.5)
    w_down = nrm(ks[19], (L, N_EXPERTS, D_EXPERT, D), DN_BETA * D_EXPERT ** -0.5)
    ln2_g = 1.0 + nrm(ks[20], (L, D), 0.01)
    ln2_b = nrm(ks[21], (L, D), 0.01)
    return {"x": x, "c": c, "positions": positions, "w_ada": w_ada, "b_ada": b_ada,
            "w_in": w_in, "conv_w": conv_w, "conv_b": conv_b, "conv_ln_g": conv_ln_g,
            "conv_ln_b": conv_ln_b, "w_out": w_out, "ln1_g": ln1_g, "ln1_b": ln1_b,
            "w_group_router": w_group_router, "b_group_router": b_group_router,
            "w_expert_router": w_expert_router, "b_expert_router": b_expert_router,
            "w_gate": w_gate, "w_up": w_up, "w_down": w_down, "ln2_g": ln2_g, "ln2_b": ln2_b}


def reference(x, c, positions, w_ada, b_ada, w_in, conv_w, conv_b, conv_ln_g, conv_ln_b, w_out,
              ln1_g, ln1_b, w_group_router, b_group_router, w_expert_router, b_expert_router,
              w_gate, w_up, w_down, ln2_g, ln2_b):
    c_act = jax.nn.silu(c)
    for l in range(DEPTH):
        mod = (c_act @ w_ada[l] + b_ada[l])[:, None, :]
        shift1, scale1, gate1, shift2, scale2, gate2 = jnp.split(mod, 6, axis=-1)
        u = layer_norm(x) * (1.0 + scale1) + shift1
        mix = token_mixer(u, positions, w_in[l], conv_w[l], conv_b[l], conv_ln_g[l],
                          conv_ln_b[l], w_out[l])
        x = layer_norm(DN_ALPHA * x + (1.0 + gate1) * mix, ln1_g[l], ln1_b[l])
        u = layer_norm(x) * (1.0 + scale2) + shift2
        ffn = hierarchical_moe(u, w_group_router[l], b_group_router[l], w_expert_router[l],
                               b_expert_router[l], w_gate[l], w_up[l], w_down[l])
        x = layer_norm(DN_ALPHA * x + (1.0 + gate2) * ffn, ln2_g[l], ln2_b[l])
    return x
```

```python
import functools
import math

import jax
import jax.numpy as jnp
from jax import lax
from jax.experimental import pallas as pl
from jax.experimental.pallas import tpu as pltpu

F32 = jnp.float32
BF16 = jnp.bfloat16

D_MODEL = 2048
D_CONV = 1024
D_RET = 1024
CONV_WIDTH = 31
RET_HEADS = 8
HEAD_DIM = 128
RET_CHUNK = 128
ROPE_BASE = 10000.0
N_GROUPS = 4
EXPERTS_PER_GROUP = 8
N_EXPERTS = 32
TOP_K = 2
D_EXPERT = 1024
LN_EPS = 1e-5
DEPTH = 1
DN_ALPHA = (2 * DEPTH) ** 0.25

LANES = 128
VMEM_LIMIT = 56 * 1024 * 1024

ADA_TN = 1024
INPROJ_TM = 512
INPROJ_TN = 1024
CONV_TT = 512
CONV_HALO = 32
CONV_ROWS = 32
RET_TT = 256
OUTPROJ_TM = 256
MOE_TM = 256
MOE_FC = 256
COMB_TM = 256


def _ln(x):
    mu = jnp.mean(x, axis=-1, keepdims=True)
    xc = x - mu
    var = jnp.mean(xc * xc, axis=-1, keepdims=True)
    return xc * lax.rsqrt(var + LN_EPS)


def _silu(x):
    return x * jax.nn.sigmoid(x)


def _params(sem):
    return pltpu.CompilerParams(dimension_semantics=sem, vmem_limit_bytes=VMEM_LIMIT)


def _ada_kernel(c_ref, w_ref, b_ref, o_ref):
    ca = _silu(c_ref[...]).astype(BF16)
    o_ref[...] = jnp.dot(ca, w_ref[...].astype(BF16), preferred_element_type=F32) + b_ref[...]


def _ada(c_pad, w_ada, b_ada):
    rows, d = c_pad.shape
    n = w_ada.shape[1]
    return pl.pallas_call(
        _ada_kernel,
        grid=(n // ADA_TN,),
        in_specs=[pl.BlockSpec((rows, d), lambda j: (0, 0)),
                  pl.BlockSpec((d, ADA_TN), lambda j: (0, j)),
                  pl.BlockSpec((1, ADA_TN), lambda j: (0, j))],
        out_specs=pl.BlockSpec((rows, ADA_TN), lambda j: (0, j)),
        out_shape=jax.ShapeDtypeStruct((rows, n), F32),
        compiler_params=_params(("arbitrary",)),
        name="ada",
    )(c_pad, w_ada, b_ada)


def _inproj_kernel(x_ref, sc_ref, sh_ref, pos_ref, invf_ref, sgn_ref, w_ref,
                   h0_ref, q_ref, k_ref, v_ref, g_ref,
                   u_scr, a_scr, cos_scr, sin_scr):
    j = pl.program_id(1)

    @pl.when(j == 0)
    def _():
        u = _ln(x_ref[...]) * (1.0 + sc_ref[...]) + sh_ref[...]
        u_scr[...] = u.astype(BF16)
        ang = pos_ref[...].astype(F32) * invf_ref[...]
        cos_scr[...] = jnp.cos(ang)
        sin_scr[...] = jnp.sin(ang) * sgn_ref[...]

    acc = jnp.dot(u_scr[...], w_ref[...], preferred_element_type=F32)

    def rotary_to(out_ref, scale):
        cos = cos_scr[...]
        sin = sin_scr[...]
        for h in range(RET_HEADS):
            xs = acc[:, h * HEAD_DIM:(h + 1) * HEAD_DIM]
            rot = xs * cos + pltpu.roll(xs, HEAD_DIM // 2, axis=1) * sin
            if scale is not None:
                rot = rot * scale
            out_ref[:, h * HEAD_DIM:(h + 1) * HEAD_DIM] = rot.astype(out_ref.dtype)

    @pl.when(j == 0)
    def _():
        a_scr[...] = acc

    @pl.when(j == 1)
    def _():
        h0_ref[...] = a_scr[...] * jax.nn.sigmoid(acc)

    @pl.when(j == 2)
    def _():
        rotary_to(q_ref, None)

    @pl.when(j == 3)
    def _():
        rotary_to(k_ref, HEAD_DIM ** -0.5)

    @pl.when(j == 4)
    def _():
        v_ref[...] = acc.astype(BF16)

    @pl.when(j == 5)
    def _():
        g_ref[...] = _silu(acc)


def _inproj(x2, scale1, shift1, pos2, invf, sgn, w_in_bf, seq):
    n, d = x2.shape
    tm, tn = INPROJ_TM, INPROJ_TN
    tiles_per_seq = seq // tm
    row = lambda i, j: (i, 0)
    per_batch = lambda i, j: (i // tiles_per_seq, 0, 0)
    const = lambda i, j: (0, 0)
    out_shape = [jax.ShapeDtypeStruct((n, D_CONV), F32),
                 jax.ShapeDtypeStruct((n, D_RET), BF16),
                 jax.ShapeDtypeStruct((n, D_RET), BF16),
                 jax.ShapeDtypeStruct((n, D_RET), BF16),
                 jax.ShapeDtypeStruct((n, D_RET), F32)]
    return pl.pallas_call(
        _inproj_kernel,
        grid=(n // tm, w_in_bf.shape[1] // tn),
        in_specs=[pl.BlockSpec((tm, d), row),
                  pl.BlockSpec((None, 1, d), per_batch),
                  pl.BlockSpec((None, 1, d), per_batch),
                  pl.BlockSpec((tm, 1), row),
                  pl.BlockSpec((1, LANES), const),
                  pl.BlockSpec((1, LANES), const),
                  pl.BlockSpec((d, tn), lambda i, j: (0, j))],
        out_specs=[pl.BlockSpec((tm, tn), row) for _ in out_shape],
        out_shape=out_shape,
        scratch_shapes=[pltpu.VMEM((tm, d), BF16),
                        pltpu.VMEM((tm, tn), F32),
                        pltpu.VMEM((tm, LANES), F32),
                        pltpu.VMEM((tm, LANES), F32)],
        compiler_params=_params(("arbitrary", "arbitrary")),
        name="inproj",
    )(x2, scale1, shift1, pos2, invf, sgn, w_in_bf)


def _conv_kernel(main_ref, halo_ref, w_ref, b_ref, g_ref, beta_ref, o_ref, buf):
    t = pl.program_id(1)
    buf[0:CONV_HALO, :] = jnp.where(t > 0, halo_ref[...], 0.0)
    buf[CONV_HALO:, :] = main_ref[...]
    first = CONV_HALO - (CONV_WIDTH - 1)
    bias = b_ref[...]
    gain = g_ref[...]
    beta = beta_ref[...]
    for ci in range(CONV_TT // CONV_ROWS):
        r0 = ci * CONV_ROWS
        acc = jnp.zeros((CONV_ROWS, D_CONV), F32) + bias
        for j in range(CONV_WIDTH):
            acc = acc + w_ref[j:j + 1, :] * buf[r0 + first + j:r0 + first + j + CONV_ROWS, :]
        y = _ln(acc) * gain + beta
        o_ref[r0:r0 + CONV_ROWS, :] = _silu(y).astype(BF16)


def _conv(h0, conv_w, conv_b, ln_g, ln_b):
    b, t, c = h0.shape
    tt = CONV_TT
    ratio = tt // CONV_HALO
    const = lambda bi, ti: (0, 0)
    return pl.pallas_call(
        _conv_kernel,
        grid=(b, t // tt),
        in_specs=[pl.BlockSpec((None, tt, c), lambda bi, ti: (bi, ti, 0)),
                  pl.BlockSpec((None, CONV_HALO, c),
                               lambda bi, ti: (bi, jnp.maximum(ti * ratio - 1, 0), 0)),
                  pl.BlockSpec((CONV_WIDTH, c), const),
                  pl.BlockSpec((1, c), const),
                  pl.BlockSpec((1, c), const),
                  pl.BlockSpec((1, c), const)],
        out_specs=pl.BlockSpec((None, tt, c), lambda bi, ti: (bi, ti, 0)),
        out_shape=jax.ShapeDtypeStruct((b, t, c), BF16),
        scratch_shapes=[pltpu.VMEM((tt + CONV_HALO, c), F32)],
        compiler_params=_params(("arbitrary", "arbitrary")),
        name="conv",
    )(h0, h0, conv_w, conv_b, ln_g, ln_b)


def _ret_kernel(q_ref, k_ref, v_ref, gs_ref, dec_ref, xi_ref, zeta_ref, cd_ref, o_ref, st):
    t = pl.program_id(1)

    @pl.when(t == 0)
    def _():
        st[...] = jnp.zeros(st.shape, F32)

    for c in range(RET_TT // RET_CHUNK):
        rows = slice(c * RET_CHUNK, (c + 1) * RET_CHUNK)
        for h in range(RET_HEADS):
            cols = slice(h * HEAD_DIM, (h + 1) * HEAD_DIM)
            q = q_ref[rows, cols]
            k = k_ref[rows, cols]
            v = v_ref[rows, cols]
            s = lax.dot_general(q, k, (((1,), (1,)), ((), ())), preferred_element_type=F32)
            s = s * dec_ref[h]
            inner = jnp.dot(s.astype(BF16), v, preferred_element_type=F32)
            state = st[h]
            qx = (q.astype(F32) * xi_ref[h]).astype(BF16)
            cross = jnp.dot(qx, state.astype(BF16), preferred_element_type=F32)
            kz = (k.astype(F32) * zeta_ref[h]).astype(BF16)
            kv = lax.dot_general(kz, v, (((0,), (0,)), ((), ())), preferred_element_type=F32)
            st[h] = state * cd_ref[h] + kv
            r = _ln(inner + cross)
            o_ref[rows, cols] = (gs_ref[rows, cols] * r).astype(BF16)


def _retention(q, k, v, gs, dec, xi, zeta, cd, batch, seq):
    n, w = q.shape
    tt = RET_TT
    tiles = seq // tt
    row = lambda bi, ti: (bi * tiles + ti, 0)
    const3 = lambda bi, ti: (0, 0, 0)
    tab = pl.BlockSpec((RET_HEADS, RET_CHUNK, HEAD_DIM), const3)
    return pl.pallas_call(
        _ret_kernel,
        grid=(batch, tiles),
        in_specs=[pl.BlockSpec((tt, w), row)] * 4 + [
            tab, tab, tab, pl.BlockSpec((RET_HEADS, 1, HEAD_DIM), const3)],
        out_specs=pl.BlockSpec((tt, w), row),
        out_shape=jax.ShapeDtypeStruct((n, w), BF16),
        scratch_shapes=[pltpu.VMEM((RET_HEADS, HEAD_DIM, HEAD_DIM), F32)],
        compiler_params=_params(("arbitrary", "arbitrary")),
        name="retention",
    )(q, k, v, gs, dec, xi, zeta, cd)


def _outproj_kernel(hc_ref, r_ref, x_ref, w_ref, gate_ref, g1_ref, b1_ref, sc2_ref, sh2_ref,
                    wr_ref, br_ref, x1_ref, u2_ref, route_ref):
    mix = jnp.dot(hc_ref[...], w_ref[0:D_CONV, :], preferred_element_type=F32)
    mix = mix + jnp.dot(r_ref[...], w_ref[D_CONV:, :], preferred_element_type=F32)
    y = DN_ALPHA * x_ref[...] + (1.0 + gate_ref[...]) * mix
    x1 = _ln(y) * g1_ref[...] + b1_ref[...]
    x1_ref[...] = x1
    u2 = _ln(x1) * (1.0 + sc2_ref[...]) + sh2_ref[...]
    u2_ref[...] = u2
    logits = jnp.dot(u2.astype(BF16), wr_ref[...].astype(BF16),
                     preferred_element_type=F32) + br_ref[...]

    lane = lax.broadcasted_iota(jnp.int32, logits.shape, 1).astype(F32)
    neg = jnp.float32(-jnp.inf)
    big = jnp.float32(LANES)
    first_where = lambda m: jnp.min(jnp.where(m, lane, big), axis=-1, keepdims=True)

    is_g = lane < N_GROUPS
    gl = jnp.where(is_g, logits, neg)
    gmax = jnp.max(gl, axis=-1, keepdims=True)
    grp = first_where(gl == gmax)
    gsum = jnp.sum(jnp.where(is_g, jnp.exp(gl - gmax), 0.0), axis=-1, keepdims=True)
    g_w = 1.0 / gsum

    lo = N_GROUPS + grp * EXPERTS_PER_GROUP
    el = jnp.where((lane >= lo) & (lane < lo + EXPERTS_PER_GROUP), logits, neg)
    m1 = jnp.max(el, axis=-1, keepdims=True)
    i1 = first_where(el == m1)
    el2 = jnp.where(lane == i1, neg, el)
    m2 = jnp.max(el2, axis=-1, keepdims=True)
    i2 = first_where(el2 == m2)
    e21 = jnp.exp(m2 - m1)
    w1 = g_w / (1.0 + e21)
    w2 = g_w * e21 / (1.0 + e21)
    route = jnp.where(lane == 0, i1 - N_GROUPS,
                      jnp.where(lane == 1, i2 - N_GROUPS,
                                jnp.where(lane == 2, w1, jnp.where(lane == 3, w2, 0.0))))
    route_ref[...] = route


def _outproj(hc, r, x2, w_out_bf, gate1, ln1_g, ln1_b, scale2, shift2, w_router, b_router, seq):
    n, d = x2.shape
    tm = OUTPROJ_TM
    tiles_per_seq = seq // tm
    row = lambda i: (i, 0)
    per_batch = lambda i: (i // tiles_per_seq, 0, 0)
    const = lambda i: (0, 0)
    vec = pl.BlockSpec((1, d), const)
    mod = pl.BlockSpec((None, 1, d), per_batch)
    return pl.pallas_call(
        _outproj_kernel,
        grid=(n // tm,),
        in_specs=[pl.BlockSpec((tm, D_CONV), row),
                  pl.BlockSpec((tm, D_RET), row),
                  pl.BlockSpec((tm, d), row),
                  pl.BlockSpec((d, d), const),
                  mod, vec, vec, mod, mod,
                  pl.BlockSpec((d, LANES), const),
                  pl.BlockSpec((1, LANES), const)],
        out_specs=[pl.BlockSpec((tm, d), row),
                   pl.BlockSpec((tm, d), row),
                   pl.BlockSpec((tm, LANES), row)],
        out_shape=[jax.ShapeDtypeStruct((n, d), F32),
                   jax.ShapeDtypeStruct((n, d), F32),
                   jax.ShapeDtypeStruct((n, LANES), F32)],
        compiler_params=_params(("arbitrary",)),
        name="outproj",
    )(hc, r, x2, w_out_bf, gate1, ln1_g, ln1_b, scale2, shift2, w_router, b_router)


def _moe_kernel(be_ref, nu_ref, idx_ref, idx_next_ref, u2_hbm, wg_ref, wu_ref, wd_ref,
                y_ref, xbuf, sem):
    i = pl.program_id(0)
    n_used = nu_ref[0]
    slot = i % 2

    def start_gather(rows_ref, s):
        for r in range(MOE_TM):
            pltpu.make_async_copy(u2_hbm.at[pl.ds(rows_ref[0, 0, r], 1)],
                                  xbuf.at[s, pl.ds(r, 1)], sem.at[s]).start()

    @pl.when(i == 0)
    def _():
        start_gather(idx_ref, 0)

    @pl.when(i < n_used)
    def _():
        pltpu.make_async_copy(u2_hbm.at[pl.ds(0, MOE_TM)], xbuf.at[slot], sem.at[slot]).wait()

        @pl.when(i + 1 < n_used)
        def _():
            start_gather(idx_next_ref, 1 - slot)

        x = xbuf[slot].astype(BF16)
        acc = jnp.zeros((MOE_TM, D_MODEL), F32)
        for c in range(D_EXPERT // MOE_FC):
            f = slice(c * MOE_FC, (c + 1) * MOE_FC)
            hg = jnp.dot(x, wg_ref[0, :, f], preferred_element_type=F32)
            hu = jnp.dot(x, wu_ref[0, :, f], preferred_element_type=F32)
            hid = (_silu(hg) * hu).astype(BF16)
            acc = acc + jnp.dot(hid, wd_ref[0, f, :], preferred_element_type=F32)
        y_ref[...] = acc

    @pl.when(i >= n_used)
    def _():
        y_ref[...] = jnp.zeros(y_ref.shape, F32)


def _moe(block_e, n_used, slot_tok3, u2, wg_bf, wu_bf, wd_bf):
    nb = slot_tok3.shape[0]
    d = u2.shape[1]
    tm = MOE_TM
    smem_idx = lambda fn: pl.BlockSpec((1, 1, tm), fn, memory_space=pltpu.SMEM)
    grid_spec = pltpu.PrefetchScalarGridSpec(
        num_scalar_prefetch=2,
        grid=(nb,),
        in_specs=[smem_idx(lambda i, be, nu: (i, 0, 0)),
                  smem_idx(lambda i, be, nu: (jnp.minimum(i + 1, nb - 1), 0, 0)),
                  pl.BlockSpec(memory_space=pl.ANY),
                  pl.BlockSpec((1, d, D_EXPERT), lambda i, be, nu: (be[i], 0, 0)),
                  pl.BlockSpec((1, d, D_EXPERT), lambda i, be, nu: (be[i], 0, 0)),
                  pl.BlockSpec((1, D_EXPERT, d), lambda i, be, nu: (be[i], 0, 0))],
        out_specs=pl.BlockSpec((tm, d), lambda i, be, nu: (i, 0)),
        scratch_shapes=[pltpu.VMEM((2, tm, d), F32),
                        pltpu.SemaphoreType.DMA((2,))],
    )
    return pl.pallas_call(
        _moe_kernel,
        grid_spec=grid_spec,
        out_shape=jax.ShapeDtypeStruct((nb * tm, d), F32),
        compiler_params=_params(("arbitrary",)),
        name="moe",
    )(block_e, n_used, slot_tok3, slot_tok3, u2, wg_bf, wu_bf, wd_bf)


def _combine_kernel(pos_ref, pos_next_ref, y_hbm, x1_ref, route_ref, gate_ref, g_ref, b_ref,
                    o_ref, ybuf, sem):
    i = pl.program_id(0)
    n_steps = pl.num_programs(0)
    slot = i % 2

    def start_gather(rows_ref, s):
        for r in range(COMB_TM):
            for kk in range(TOP_K):
                pltpu.make_async_copy(y_hbm.at[pl.ds(rows_ref[0, 0, TOP_K * r + kk], 1)],
                                      ybuf.at[s, kk, pl.ds(r, 1)], sem.at[s]).start()

    @pl.when(i == 0)
    def _():
        start_gather(pos_ref, 0)

    for kk in range(TOP_K):
        pltpu.make_async_copy(y_hbm.at[pl.ds(0, COMB_TM)], ybuf.at[slot, kk], sem.at[slot]).wait()

    @pl.when(i + 1 < n_steps)
    def _():
        start_gather(pos_next_ref, 1 - slot)

    w1 = route_ref[:, 2:3]
    w2 = route_ref[:, 3:4]
    ffn = w1 * ybuf[slot, 0] + w2 * ybuf[slot, 1]
    y = DN_ALPHA * x1_ref[...] + (1.0 + gate_ref[...]) * ffn
    o_ref[...] = _ln(y) * g_ref[...] + b_ref[...]


def _combine(pos3, y, x1, route, gate2, ln2_g, ln2_b, seq):
    n, d = x1.shape
    tm = COMB_TM
    steps = n // tm
    tiles_per_seq = seq // tm
    row = lambda i: (i, 0)
    const = lambda i: (0, 0)
    smem_idx = lambda fn: pl.BlockSpec((1, 1, TOP_K * tm), fn, memory_space=pltpu.SMEM)
    return pl.pallas_call(
        _combine_kernel,
        grid=(steps,),
        in_specs=[smem_idx(lambda i: (i, 0, 0)),
                  smem_idx(lambda i: (jnp.minimum(i + 1, steps - 1), 0, 0)),
                  pl.BlockSpec(memory_space=pl.ANY),
                  pl.BlockSpec((tm, d), row),
                  pl.BlockSpec((tm, LANES), row),
                  pl.BlockSpec((None, 1, d), lambda i: (i // tiles_per_seq, 0, 0)),
                  pl.BlockSpec((1, d), const),
                  pl.BlockSpec((1, d), const)],
        out_specs=pl.BlockSpec((tm, d), row),
        out_shape=jax.ShapeDtypeStruct((n, d), F32),
        scratch_shapes=[pltpu.VMEM((2, TOP_K, tm, d), F32),
                        pltpu.SemaphoreType.DMA((2,))],
        compiler_params=_params(("arbitrary",)),
        name="combine",
    )(pos3, pos3, y, x1, route, gate2, ln2_g, ln2_b)


def _retention_tables():
    h = jnp.arange(RET_HEADS, dtype=F32)
    log_gamma = jnp.log1p(-jnp.exp2(-5.0 - h))
    idx = jnp.arange(RET_CHUNK, dtype=F32)
    diff = idx[:, None] - idx[None, :]
    causal = diff >= 0
    dec = jnp.where(causal[None],
                    jnp.exp(jnp.where(causal, diff, 0.0)[None] * log_gamma[:, None, None]), 0.0)
    xi = jnp.exp((idx[None, :] + 1.0) * log_gamma[:, None])
    zeta = jnp.exp((RET_CHUNK - 1.0 - idx[None, :]) * log_gamma[:, None])
    cd = jnp.exp(RET_CHUNK * log_gamma)
    bc = lambda a: jnp.broadcast_to(a[:, :, None], (RET_HEADS, RET_CHUNK, HEAD_DIM))
    return dec, bc(xi), bc(zeta), jnp.broadcast_to(cd[:, None, None], (RET_HEADS, 1, HEAD_DIM))


def _rope_tables():
    half = HEAD_DIM // 2
    inv_freq = jnp.exp(-math.log(ROPE_BASE) * jnp.arange(half, dtype=F32) / half)
    invf = jnp.concatenate([inv_freq, inv_freq])[None, :]
    sgn = jnp.concatenate([-jnp.ones((half,), F32), jnp.ones((half,), F32)])[None, :]
    return invf, sgn


def _dispatch_plan(expert_ids, n_tok):
    tm = MOE_TM
    n_slots = n_tok * TOP_K
    cap = n_slots + N_EXPERTS * tm
    nb = cap // tm
    flat_e = expert_ids.reshape(-1)
    counts = jnp.bincount(flat_e, length=N_EXPERTS)
    padded = (counts + tm - 1) // tm * tm
    pad_end = jnp.cumsum(padded)
    pad_start = pad_end - padded
    start = jnp.cumsum(counts) - counts
    order = jnp.argsort(flat_e)
    e_sorted = flat_e[order]
    dest = (pad_start[e_sorted] + jnp.arange(n_slots) - start[e_sorted]).astype(jnp.int32)
    slot_tok = jnp.zeros((cap,), jnp.int32).at[dest].set((order // TOP_K).astype(jnp.int32))
    pos = jnp.zeros((n_slots,), jnp.int32).at[order].set(dest)
    n_used = (pad_end[-1] // tm).astype(jnp.int32)
    blocks = jnp.arange(nb, dtype=jnp.int32)
    block_e = jnp.minimum(jnp.searchsorted(pad_end, blocks * tm, side='right'),
                          N_EXPERTS - 1).astype(jnp.int32)
    block_e = jnp.where(blocks < n_used, block_e, block_e[n_used - 1])
    return block_e, n_used.reshape(1), slot_tok.reshape(nb, 1, tm), pos


def kernel(x, c, positions, w_ada, b_ada, w_in, conv_w, conv_b, conv_ln_g, conv_ln_b, w_out,
           ln1_g, ln1_b, w_group_router, b_group_router, w_expert_router, b_expert_router,
           w_gate, w_up, w_down, ln2_g, ln2_b):
    batch, seq, d = x.shape
    n_tok = batch * seq
    l = 0
    row = lambda a: a[l][None, :]

    c_pad = jnp.zeros((8, d), F32).at[:batch].set(c)
    mod = _ada(c_pad, w_ada[l], b_ada[l][None, :])[:batch]
    shift1, scale1, gate1, shift2, scale2, gate2 = [m[:, None, :] for m in jnp.split(mod, 6, axis=-1)]

    x2 = x.reshape(n_tok, d)
    invf, sgn = _rope_tables()
    h0, q, k, v, gs = _inproj(x2, scale1, shift1, positions.reshape(n_tok, 1), invf, sgn,
                              w_in[l].astype(BF16), seq)

    hc = _conv(h0.reshape(batch, seq, D_CONV), conv_w[l], row(conv_b), row(conv_ln_g),
               row(conv_ln_b)).reshape(n_tok, D_CONV)

    dec, xi, zeta, cd = _retention_tables()
    r = _retention(q, k, v, gs, dec, xi, zeta, cd, batch, seq)

    n_route = N_GROUPS + N_EXPERTS
    w_router = jnp.zeros((d, LANES), F32).at[:, :N_GROUPS].set(w_group_router[l])
    w_router = w_router.at[:, N_GROUPS:n_route].set(w_expert_router[l])
    b_router = jnp.zeros((1, LANES), F32).at[0, :N_GROUPS].set(b_group_router[l])
    b_router = b_router.at[0, N_GROUPS:n_route].set(b_expert_router[l])
    x1, u2, route = _outproj(hc, r, x2, w_out[l].astype(BF16), gate1, row(ln1_g), row(ln1_b),
                             scale2, shift2, w_router, b_router, seq)

    expert_ids = route[:, :TOP_K].astype(jnp.int32)
    block_e, n_used, slot_tok3, pos = _dispatch_plan(expert_ids, n_tok)
    y = _moe(block_e, n_used, slot_tok3, u2, w_gate[l].astype(BF16), w_up[l].astype(BF16),
             w_down[l].astype(BF16))

    pos3 = pos.reshape(n_tok // COMB_TM, 1, TOP_K * COMB_TM)
    out = _combine(pos3, y, x1, route, gate2, row(ln2_g), row(ln2_b), seq)
    return out.reshape(batch, seq, d)
```

```python
import functools
import math

import jax
import jax.numpy as jnp
from jax import lax
from jax.experimental import pallas as pl
from jax.experimental.pallas import tpu as pltpu

F32 = jnp.float32
BF16 = jnp.bfloat16

D_MODEL = 2048
D_CONV = 1024
D_RET = 1024
CONV_WIDTH = 31
RET_HEADS = 8
HEAD_DIM = 128
RET_CHUNK = 128
ROPE_BASE = 10000.0
N_GROUPS = 4
EXPERTS_PER_GROUP = 8
N_EXPERTS = 32
TOP_K = 2
D_EXPERT = 1024
LN_EPS = 1e-5
DEPTH = 1
DN_ALPHA = (2 * DEPTH) ** 0.25

LANES = 128
SUBLANES = 8
VMEM_LIMIT = 56 * 1024 * 1024

ADA_TN = 1024
INPROJ_TM = 512
INPROJ_TN = 1024
CONV_TT = 512
CONV_HALO = 32
CONV_ROWS = 64
CONV_LANES = 256
RET_TT = 256
OUTPROJ_TM = 256
PLAN_TB = 512
MOE_TM = 256
MOE_FC = 256
COMB_TM = 256


def _ln(x):
    mu = jnp.mean(x, axis=-1, keepdims=True)
    xc = x - mu
    var = jnp.mean(xc * xc, axis=-1, keepdims=True)
    return xc * lax.rsqrt(var + LN_EPS)


def _silu(x):
    return x * jax.nn.sigmoid(x)


def _params(sem):
    return pltpu.CompilerParams(dimension_semantics=sem, vmem_limit_bytes=VMEM_LIMIT)


def _ada_kernel(c_ref, w_ref, b_ref, o_ref):
    ca = _silu(c_ref[...]).astype(BF16)
    o_ref[...] = jnp.dot(ca, w_ref[...].astype(BF16), preferred_element_type=F32) + b_ref[...]


def _ada(c_pad, w_ada, b_ada):
    rows, d = c_pad.shape
    n = w_ada.shape[1]
    return pl.pallas_call(
        _ada_kernel,
        grid=(n // ADA_TN,),
        in_specs=[pl.BlockSpec((rows, d), lambda j: (0, 0)),
                  pl.BlockSpec((d, ADA_TN), lambda j: (0, j)),
                  pl.BlockSpec((1, ADA_TN), lambda j: (0, j))],
        out_specs=pl.BlockSpec((rows, ADA_TN), lambda j: (0, j)),
        out_shape=jax.ShapeDtypeStruct((rows, n), F32),
        compiler_params=_params(("arbitrary",)),
        name="ada",
    )(c_pad, w_ada, b_ada)


def _inproj_kernel(x_ref, sc_ref, sh_ref, pos_ref, invf_ref, sgn_ref, w_ref,
                   h0_ref, q_ref, k_ref, v_ref, g_ref,
                   u_scr, a_scr, cos_scr, sin_scr):
    j = pl.program_id(1)

    @pl.when(j == 0)
    def _():
        u = _ln(x_ref[...]) * (1.0 + sc_ref[...]) + sh_ref[...]
        u_scr[...] = u.astype(BF16)
        ang = pos_ref[...].astype(F32) * invf_ref[...]
        cos_scr[...] = jnp.cos(ang)
        sin_scr[...] = jnp.sin(ang) * sgn_ref[...]

    acc = jnp.dot(u_scr[...], w_ref[...], preferred_element_type=F32)

    def rotary_to(out_ref, scale):
        cos = cos_scr[...]
        sin = sin_scr[...]
        for h in range(RET_HEADS):
            xs = acc[:, h * HEAD_DIM:(h + 1) * HEAD_DIM]
            rot = xs * cos + pltpu.roll(xs, HEAD_DIM // 2, axis=1) * sin
            if scale is not None:
                rot = rot * scale
            out_ref[:, h * HEAD_DIM:(h + 1) * HEAD_DIM] = rot.astype(out_ref.dtype)

    @pl.when(j == 0)
    def _():
        a_scr[...] = acc

    @pl.when(j == 1)
    def _():
        h0_ref[...] = a_scr[...] * jax.nn.sigmoid(acc)

    @pl.when(j == 2)
    def _():
        rotary_to(q_ref, None)

    @pl.when(j == 3)
    def _():
        rotary_to(k_ref, HEAD_DIM ** -0.5)

    @pl.when(j == 4)
    def _():
        v_ref[...] = acc.astype(BF16)

    @pl.when(j == 5)
    def _():
        g_ref[...] = _silu(acc)


def _inproj(x2, scale1, shift1, pos2, invf, sgn, w_in_bf, seq):
    n, d = x2.shape
    tm, tn = INPROJ_TM, INPROJ_TN
    tiles_per_seq = seq // tm
    row = lambda i, j: (i, 0)
    per_batch = lambda i, j: (i // tiles_per_seq, 0, 0)
    const = lambda i, j: (0, 0)
    out_shape = [jax.ShapeDtypeStruct((n, D_CONV), F32),
                 jax.ShapeDtypeStruct((n, D_RET), BF16),
                 jax.ShapeDtypeStruct((n, D_RET), BF16),
                 jax.ShapeDtypeStruct((n, D_RET), BF16),
                 jax.ShapeDtypeStruct((n, D_RET), F32)]
    return pl.pallas_call(
        _inproj_kernel,
        grid=(n // tm, w_in_bf.shape[1] // tn),
        in_specs=[pl.BlockSpec((tm, d), row),
                  pl.BlockSpec((None, 1, d), per_batch),
                  pl.BlockSpec((None, 1, d), per_batch),
                  pl.BlockSpec((tm, 1), row),
                  pl.BlockSpec((1, LANES), const),
                  pl.BlockSpec((1, LANES), const),
                  pl.BlockSpec((d, tn), lambda i, j: (0, j))],
        out_specs=[pl.BlockSpec((tm, tn), row) for _ in out_shape],
        out_shape=out_shape,
        scratch_shapes=[pltpu.VMEM((tm, d), BF16),
                        pltpu.VMEM((tm, tn), F32),
                        pltpu.VMEM((tm, LANES), F32),
                        pltpu.VMEM((tm, LANES), F32)],
        compiler_params=_params(("arbitrary", "arbitrary")),
        name="inproj",
    )(x2, scale1, shift1, pos2, invf, sgn, w_in_bf)


def _conv_kernel(main_ref, halo_ref, w_ref, b_ref, g_ref, beta_ref, o_ref, buf, cbuf):
    t = pl.program_id(1)
    buf[0:CONV_HALO, :] = jnp.where(t > 0, halo_ref[...], 0.0)
    buf[CONV_HALO:, :] = main_ref[...]
    first = CONV_HALO - (CONV_WIDTH - 1)
    rows = CONV_ROWS

    def chunk(ci, carry):
        r0 = pl.multiple_of(ci * rows, rows)
        for lg in range(D_CONV // CONV_LANES):
            lanes = slice(lg * CONV_LANES, (lg + 1) * CONV_LANES)
            acc = None
            for s in range(SUBLANES):
                taps = [j for j in range(CONV_WIDTH) if (first + j) % SUBLANES == s]
                span = rows if s == 0 else rows + SUBLANES
                part = None
                for j in taps:
                    a0 = first + j - s
                    term = w_ref[j:j + 1, lanes] * buf[pl.ds(r0 + a0, span), lanes]
                    part = term if part is None else part + term
                part = part[s:s + rows, :]
                acc = part if acc is None else acc + part
            cbuf[pl.ds(r0, rows), lanes] = acc
        return carry

    lax.fori_loop(0, CONV_TT // rows, chunk, 0)
    y = _ln(cbuf[...] + b_ref[...]) * g_ref[...] + beta_ref[...]
    o_ref[...] = _silu(y).astype(BF16)


def _conv(h0, conv_w, conv_b, ln_g, ln_b):
    b, t, c = h0.shape
    tt = CONV_TT
    ratio = tt // CONV_HALO
    const = lambda bi, ti: (0, 0)
    return pl.pallas_call(
        _conv_kernel,
        grid=(b, t // tt),
        in_specs=[pl.BlockSpec((None, tt, c), lambda bi, ti: (bi, ti, 0)),
                  pl.BlockSpec((None, CONV_HALO, c),
                               lambda bi, ti: (bi, jnp.maximum(ti * ratio - 1, 0), 0)),
                  pl.BlockSpec((CONV_WIDTH, c), const),
                  pl.BlockSpec((1, c), const),
                  pl.BlockSpec((1, c), const),
                  pl.BlockSpec((1, c), const)],
        out_specs=pl.BlockSpec((None, tt, c), lambda bi, ti: (bi, ti, 0)),
        out_shape=jax.ShapeDtypeStruct((b, t, c), BF16),
        scratch_shapes=[pltpu.VMEM((tt + CONV_HALO, c), F32),
                        pltpu.VMEM((tt, c), F32)],
        compiler_params=_params(("arbitrary", "arbitrary")),
        name="conv",
    )(h0, h0, conv_w, conv_b, ln_g, ln_b)


def _ret_kernel(q_ref, k_ref, v_ref, gs_ref, dec_ref, xi_ref, zeta_ref, cd_ref, o_ref, st):
    t = pl.program_id(1)

    @pl.when(t == 0)
    def _():
        st[...] = jnp.zeros(st.shape, F32)

    for c in range(RET_TT // RET_CHUNK):
        rows = slice(c * RET_CHUNK, (c + 1) * RET_CHUNK)
        for h in range(RET_HEADS):
            cols = slice(h * HEAD_DIM, (h + 1) * HEAD_DIM)
            q = q_ref[rows, cols]
            k = k_ref[rows, cols]
            v = v_ref[rows, cols]
            s = lax.dot_general(q, k, (((1,), (1,)), ((), ())), preferred_element_type=F32)
            s = s * dec_ref[h]
            inner = jnp.dot(s.astype(BF16), v, preferred_element_type=F32)
            state = st[h]
            qx = (q.astype(F32) * xi_ref[h]).astype(BF16)
            cross = jnp.dot(qx, state.astype(BF16), preferred_element_type=F32)
            kz = (k.astype(F32) * zeta_ref[h]).astype(BF16)
            kv = lax.dot_general(kz, v, (((0,), (0,)), ((), ())), preferred_element_type=F32)
            st[h] = state * cd_ref[h] + kv
            r = _ln(inner + cross)
            o_ref[rows, cols] = (gs_ref[rows, cols] * r).astype(BF16)


def _retention(q, k, v, gs, dec, xi, zeta, cd, batch, seq):
    n, w = q.shape
    tt = RET_TT
    tiles = seq // tt
    row = lambda bi, ti: (bi * tiles + ti, 0)
    const3 = lambda bi, ti: (0, 0, 0)
    tab = pl.BlockSpec((RET_HEADS, RET_CHUNK, HEAD_DIM), const3)
    return pl.pallas_call(
        _ret_kernel,
        grid=(batch, tiles),
        in_specs=[pl.BlockSpec((tt, w), row)] * 4 + [
            tab, tab, tab, pl.BlockSpec((RET_HEADS, 1, HEAD_DIM), const3)],
        out_specs=pl.BlockSpec((tt, w), row),
        out_shape=jax.ShapeDtypeStruct((n, w), BF16),
        scratch_shapes=[pltpu.VMEM((RET_HEADS, HEAD_DIM, HEAD_DIM), F32)],
        compiler_params=_params(("arbitrary", "arbitrary")),
        name="retention",
    )(q, k, v, gs, dec, xi, zeta, cd)


def _outproj_kernel(hc_ref, r_ref, x_ref, w_ref, gate_ref, g1_ref, b1_ref, sc2_ref, sh2_ref,
                    wr_ref, br_ref, x1_ref, u2_ref, route_ref):
    mix = jnp.dot(hc_ref[...], w_ref[0:D_CONV, :], preferred_element_type=F32)
    mix = mix + jnp.dot(r_ref[...], w_ref[D_CONV:, :], preferred_element_type=F32)
    y = DN_ALPHA * x_ref[...] + (1.0 + gate_ref[...]) * mix
    x1 = _ln(y) * g1_ref[...] + b1_ref[...]
    x1_ref[...] = x1
    u2 = _ln(x1) * (1.0 + sc2_ref[...]) + sh2_ref[...]
    u2_ref[...] = u2
    logits = jnp.dot(u2.astype(BF16), wr_ref[...].astype(BF16),
                     preferred_element_type=F32) + br_ref[...]

    lane = lax.broadcasted_iota(jnp.int32, logits.shape, 1).astype(F32)
    neg = jnp.float32(-jnp.inf)
    big = jnp.float32(LANES)
    first_where = lambda m: jnp.min(jnp.where(m, lane, big), axis=-1, keepdims=True)

    is_g = lane < N_GROUPS
    gl = jnp.where(is_g, logits, neg)
    gmax = jnp.max(gl, axis=-1, keepdims=True)
    grp = first_where(gl == gmax)
    gsum = jnp.sum(jnp.where(is_g, jnp.exp(gl - gmax), 0.0), axis=-1, keepdims=True)
    g_w = 1.0 / gsum

    lo = N_GROUPS + grp * EXPERTS_PER_GROUP
    el = jnp.where((lane >= lo) & (lane < lo + EXPERTS_PER_GROUP), logits, neg)
    m1 = jnp.max(el, axis=-1, keepdims=True)
    i1 = first_where(el == m1)
    el2 = jnp.where(lane == i1, neg, el)
    m2 = jnp.max(el2, axis=-1, keepdims=True)
    i2 = first_where(el2 == m2)
    e21 = jnp.exp(m2 - m1)
    w1 = g_w / (1.0 + e21)
    w2 = g_w * e21 / (1.0 + e21)
    route = jnp.where(lane == 0, i1 - N_GROUPS,
                      jnp.where(lane == 1, i2 - N_GROUPS,
                                jnp.where(lane == 2, w1, jnp.where(lane == 3, w2, 0.0))))
    route_ref[...] = route


def _outproj(hc, r, x2, w_out_bf, gate1, ln1_g, ln1_b, scale2, shift2, w_router, b_router, seq):
    n, d = x2.shape
    tm = OUTPROJ_TM
    tiles_per_seq = seq // tm
    row = lambda i: (i, 0)
    per_batch = lambda i: (i // tiles_per_seq, 0, 0)
    const = lambda i: (0, 0)
    vec = pl.BlockSpec((1, d), const)
    mod = pl.BlockSpec((None, 1, d), per_batch)
    return pl.pallas_call(
        _outproj_kernel,
        grid=(n // tm,),
        in_specs=[pl.BlockSpec((tm, D_CONV), row),
                  pl.BlockSpec((tm, D_RET), row),
                  pl.BlockSpec((tm, d), row),
                  pl.BlockSpec((d, d), const),
                  mod, vec, vec, mod, mod,
                  pl.BlockSpec((d, LANES), const),
                  pl.BlockSpec((1, LANES), const)],
        out_specs=[pl.BlockSpec((tm, d), row),
                   pl.BlockSpec((tm, d), row),
                   pl.BlockSpec((tm, LANES), row)],
        out_shape=[jax.ShapeDtypeStruct((n, d), F32),
                   jax.ShapeDtypeStruct((n, d), F32),
                   jax.ShapeDtypeStruct((n, LANES), F32)],
        compiler_params=_params(("arbitrary",)),
        name="outproj",
    )(hc, r, x2, w_out_bf, gate1, ln1_g, ln1_b, scale2, shift2, w_router, b_router)


def _plan_kernel(route_ref, pos_ref, meta_ref, tri, upper, run, pstart):
    ph = pl.program_id(0)
    i = pl.program_id(1)
    tb = PLAN_TB
    lane = lax.broadcasted_iota(jnp.int32, (tb, LANES), 1).astype(F32)
    e1 = route_ref[:, 0:1]
    e2 = route_ref[:, 1:2]
    member = (lane == e1) | (lane == e2)
    col_count = jnp.sum(member.astype(F32), axis=0, keepdims=True)

    @pl.when((ph == 0) & (i == 0))
    def _():
        r = lax.broadcasted_iota(jnp.int32, (tb, tb), 0)
        c = lax.broadcasted_iota(jnp.int32, (tb, tb), 1)
        tri[...] = (c < r).astype(BF16)
        ru = lax.broadcasted_iota(jnp.int32, (LANES, LANES), 0)
        cu = lax.broadcasted_iota(jnp.int32, (LANES, LANES), 1)
        upper[...] = (ru < cu).astype(BF16)
        run[...] = jnp.zeros(run.shape, F32)

    @pl.when(ph == 0)
    def _():
        run[...] += col_count

    @pl.when((ph == 0) & (i == pl.num_programs(1) - 1))
    def _():
        counts = run[...]
        blocks = jnp.floor((counts + (MOE_TM - 1)) * (1.0 / MOE_TM))
        start_blocks = jnp.dot(jnp.broadcast_to(blocks, (SUBLANES, LANES)).astype(BF16), upper[...],
                               preferred_element_type=F32)[0:1, :]
        pstart[...] = start_blocks * MOE_TM
        row = lax.broadcasted_iota(jnp.int32, (SUBLANES, LANES), 0)
        meta_ref[...] = jnp.where(row == 0, counts, jnp.where(row == 1, blocks,
                                  jnp.where(row == 2, start_blocks, 0.0)))
        run[...] = jnp.zeros(run.shape, F32)

    @pl.when(ph == 1)
    def _():
        earlier = jnp.dot(tri[...], member.astype(BF16), preferred_element_type=F32) + run[...]
        dest = earlier + pstart[...]
        p1 = jnp.sum(jnp.where(lane == e1, dest, 0.0), axis=-1, keepdims=True)
        p2 = jnp.sum(jnp.where(lane == e2, dest, 0.0), axis=-1, keepdims=True)
        pos_ref[...] = jnp.where(lane == 0, p1, jnp.where(lane == 1, p2, 0.0)).astype(jnp.int32)
        run[...] += col_count


def _plan(route):
    n = route.shape[0]
    tb = PLAN_TB
    return pl.pallas_call(
        _plan_kernel,
        grid=(2, n // tb),
        in_specs=[pl.BlockSpec((tb, LANES), lambda ph, i: (i, 0))],
        out_specs=[pl.BlockSpec((tb, LANES), lambda ph, i: (ph * i, 0)),
                   pl.BlockSpec((SUBLANES, LANES), lambda ph, i: (0, 0))],
        out_shape=[jax.ShapeDtypeStruct((n, LANES), jnp.int32),
                   jax.ShapeDtypeStruct((SUBLANES, LANES), F32)],
        scratch_shapes=[pltpu.VMEM((tb, tb), BF16),
                        pltpu.VMEM((LANES, LANES), BF16),
                        pltpu.VMEM((1, LANES), F32),
                        pltpu.VMEM((1, LANES), F32)],
        compiler_params=_params(("arbitrary", "arbitrary")),
        name="plan",
    )(route)


def _moe_kernel(be_ref, nu_ref, idx_ref, idx_next_ref, u2_hbm, wg_ref, wu_ref, wd_ref,
                y_ref, xbuf, sem):
    i = pl.program_id(0)
    n_used = nu_ref[0]
    slot = i % 2

    def start_gather(rows_ref, s):
        for r in range(MOE_TM):
            pltpu.make_async_copy(u2_hbm.at[pl.ds(rows_ref[0, 0, r], 1)],
                                  xbuf.at[s, pl.ds(r, 1)], sem.at[s]).start()

    @pl.when(i == 0)
    def _():
        start_gather(idx_ref, 0)

    @pl.when(i < n_used)
    def _():
        pltpu.make_async_copy(u2_hbm.at[pl.ds(0, MOE_TM)], xbuf.at[slot], sem.at[slot]).wait()

        @pl.when(i + 1 < n_used)
        def _():
            start_gather(idx_next_ref, 1 - slot)

        x = xbuf[slot].astype(BF16)
        acc = jnp.zeros((MOE_TM, D_MODEL), F32)
        for c in range(D_EXPERT // MOE_FC):
            f = slice(c * MOE_FC, (c + 1) * MOE_FC)
            hg = jnp.dot(x, wg_ref[0, :, f], preferred_element_type=F32)
            hu = jnp.dot(x, wu_ref[0, :, f], preferred_element_type=F32)
            hid = (_silu(hg) * hu).astype(BF16)
            acc = acc + jnp.dot(hid, wd_ref[0, f, :], preferred_element_type=F32)
        y_ref[...] = acc

    @pl.when(i >= n_used)
    def _():
        y_ref[...] = jnp.zeros(y_ref.shape, F32)


def _moe(block_e, n_used, slot_tok3, u2, wg_bf, wu_bf, wd_bf):
    nb = slot_tok3.shape[0]
    d = u2.shape[1]
    tm = MOE_TM
    smem_idx = lambda fn: pl.BlockSpec((1, 1, tm), fn, memory_space=pltpu.SMEM)
    expert = lambda i, be, nu: (be[i], 0, 0)
    grid_spec = pltpu.PrefetchScalarGridSpec(
        num_scalar_prefetch=2,
        grid=(nb,),
        in_specs=[smem_idx(lambda i, be, nu: (i, 0, 0)),
                  smem_idx(lambda i, be, nu: (jnp.minimum(i + 1, nb - 1), 0, 0)),
                  pl.BlockSpec(memory_space=pl.ANY),
                  pl.BlockSpec((1, d, D_EXPERT), expert),
                  pl.BlockSpec((1, d, D_EXPERT), expert),
                  pl.BlockSpec((1, D_EXPERT, d), expert)],
        out_specs=pl.BlockSpec((tm, d), lambda i, be, nu: (i, 0)),
        scratch_shapes=[pltpu.VMEM((2, tm, d), F32),
                        pltpu.SemaphoreType.DMA((2,))],
    )
    return pl.pallas_call(
        _moe_kernel,
        grid_spec=grid_spec,
        out_shape=jax.ShapeDtypeStruct((nb * tm, d), F32),
        compiler_params=_params(("arbitrary",)),
        name="moe",
    )(block_e, n_used, slot_tok3, slot_tok3, u2, wg_bf, wu_bf, wd_bf)


def _combine_kernel(pos_ref, pos_next_ref, y_hbm, x1_ref, route_ref, gate_ref, g_ref, b_ref,
                    o_ref, ybuf, sem):
    i = pl.program_id(0)
    n_steps = pl.num_programs(0)
    slot = i % 2

    def start_gather(rows_ref, s):
        for r in range(COMB_TM):
            for kk in range(TOP_K):
                pltpu.make_async_copy(y_hbm.at[pl.ds(rows_ref[0, 0, TOP_K * r + kk], 1)],
                                      ybuf.at[s, kk, pl.ds(r, 1)], sem.at[s]).start()

    @pl.when(i == 0)
    def _():
        start_gather(pos_ref, 0)

    for kk in range(TOP_K):
        pltpu.make_async_copy(y_hbm.at[pl.ds(0, COMB_TM)], ybuf.at[slot, kk], sem.at[slot]).wait()

    @pl.when(i + 1 < n_steps)
    def _():
        start_gather(pos_next_ref, 1 - slot)

    w1 = route_ref[:, 2:3]
    w2 = route_ref[:, 3:4]
    ffn = w1 * ybuf[slot, 0] + w2 * ybuf[slot, 1]
    y = DN_ALPHA * x1_ref[...] + (1.0 + gate_ref[...]) * ffn
    o_ref[...] = _ln(y) * g_ref[...] + b_ref[...]


def _combine(pos3, y, x1, route, gate2, ln2_g, ln2_b, seq):
    n, d = x1.shape
    tm = COMB_TM
    steps = n // tm
    tiles_per_seq = seq // tm
    row = lambda i: (i, 0)
    const = lambda i: (0, 0)
    smem_idx = lambda fn: pl.BlockSpec((1, 1, TOP_K * tm), fn, memory_space=pltpu.SMEM)
    return pl.pallas_call(
        _combine_kernel,
        grid=(steps,),
        in_specs=[smem_idx(lambda i: (i, 0, 0)),
                  smem_idx(lambda i: (jnp.minimum(i + 1, steps - 1), 0, 0)),
                  pl.BlockSpec(memory_space=pl.ANY),
                  pl.BlockSpec((tm, d), row),
                  pl.BlockSpec((tm, LANES), row),
                  pl.BlockSpec((None, 1, d), lambda i: (i // tiles_per_seq, 0, 0)),
                  pl.BlockSpec((1, d), const),
                  pl.BlockSpec((1, d), const)],
        out_specs=pl.BlockSpec((tm, d), row),
        out_shape=jax.ShapeDtypeStruct((n, d), F32),
        scratch_shapes=[pltpu.VMEM((2, TOP_K, tm, d), F32),
                        pltpu.SemaphoreType.DMA((2,))],
        compiler_params=_params(("arbitrary",)),
        name="combine",
    )(pos3, pos3, y, x1, route, gate2, ln2_g, ln2_b)


def _retention_tables():
    h = jnp.arange(RET_HEADS, dtype=F32)
    log_gamma = jnp.log1p(-jnp.exp2(-5.0 - h))
    idx = jnp.arange(RET_CHUNK, dtype=F32)
    diff = idx[:, None] - idx[None, :]
    causal = diff >= 0
    dec = jnp.where(causal[None],
                    jnp.exp(jnp.where(causal, diff, 0.0)[None] * log_gamma[:, None, None]), 0.0)
    xi = jnp.exp((idx[None, :] + 1.0) * log_gamma[:, None])
    zeta = jnp.exp((RET_CHUNK - 1.0 - idx[None, :]) * log_gamma[:, None])
    cd = jnp.exp(RET_CHUNK * log_gamma)
    bc = lambda a: jnp.broadcast_to(a[:, :, None], (RET_HEADS, RET_CHUNK, HEAD_DIM))
    return dec, bc(xi), bc(zeta), jnp.broadcast_to(cd[:, None, None], (RET_HEADS, 1, HEAD_DIM))


def _rope_tables():
    half = HEAD_DIM // 2
    inv_freq = jnp.exp(-math.log(ROPE_BASE) * jnp.arange(half, dtype=F32) / half)
    invf = jnp.concatenate([inv_freq, inv_freq])[None, :]
    sgn = jnp.concatenate([-jnp.ones((half,), F32), jnp.ones((half,), F32)])[None, :]
    return invf, sgn


def _block_tables(meta, expert_ids, n_tok, nb):
    counts = meta[0, :N_EXPERTS].astype(jnp.int32)
    blocks = meta[1, :N_EXPERTS].astype(jnp.int32)
    first_block = meta[2, :N_EXPERTS].astype(jnp.int32)
    n_used = jnp.sum(blocks)
    end_block = first_block + blocks
    b_eff = jnp.minimum(jnp.arange(nb, dtype=jnp.int32), n_used - 1)
    block_e = jnp.minimum(jnp.sum((end_block[None, :] <= b_eff[:, None]).astype(jnp.int32), axis=1),
                          N_EXPERTS - 1)
    n_slots = n_tok * TOP_K
    fill = blocks * MOE_TM - counts
    e_iota = jnp.arange(N_EXPERTS, dtype=jnp.int32)[:, None]
    r_iota = jnp.arange(MOE_TM, dtype=jnp.int32)[None, :]
    filler_keys = jnp.where(r_iota < fill[:, None], e_iota, N_EXPERTS).reshape(-1)
    order = jnp.argsort(jnp.concatenate([expert_ids.reshape(-1), filler_keys])).astype(jnp.int32)
    slot_tok = jnp.where(order < n_slots, order // TOP_K, 0)
    return block_e, n_used.reshape(1), slot_tok.reshape(nb, 1, MOE_TM)


def kernel(x, c, positions, w_ada, b_ada, w_in, conv_w, conv_b, conv_ln_g, conv_ln_b, w_out,
           ln1_g, ln1_b, w_group_router, b_group_router, w_expert_router, b_expert_router,
           w_gate, w_up, w_down, ln2_g, ln2_b):
    batch, seq, d = x.shape
    n_tok = batch * seq
    l = 0
    row = lambda a: a[l][None, :]

    c_pad = jnp.zeros((8, d), F32).at[:batch].set(c)
    mod = _ada(c_pad, w_ada[l], b_ada[l][None, :])[:batch]
    shift1, scale1, gate1, shift2, scale2, gate2 = [m[:, None, :] for m in jnp.split(mod, 6, axis=-1)]

    x2 = x.reshape(n_tok, d)
    invf, sgn = _rope_tables()
    h0, q, k, v, gs = _inproj(x2, scale1, shift1, positions.reshape(n_tok, 1), invf, sgn,
                              w_in[l].astype(BF16), seq)

    hc = _conv(h0.reshape(batch, seq, D_CONV), conv_w[l], row(conv_b), row(conv_ln_g),
               row(conv_ln_b)).reshape(n_tok, D_CONV)

    dec, xi, zeta, cd = _retention_tables()
    r = _retention(q, k, v, gs, dec, xi, zeta, cd, batch, seq)

    n_route = N_GROUPS + N_EXPERTS
    w_router = jnp.zeros((d, LANES), F32).at[:, :N_GROUPS].set(w_group_router[l])
    w_router = w_router.at[:, N_GROUPS:n_route].set(w_expert_router[l])
    b_router = jnp.zeros((1, LANES), F32).at[0, :N_GROUPS].set(b_group_router[l])
    b_router = b_router.at[0, N_GROUPS:n_route].set(b_expert_router[l])
    x1, u2, route = _outproj(hc, r, x2, w_out[l].astype(BF16), gate1, row(ln1_g), row(ln1_b),
                             scale2, shift2, w_router, b_router, seq)

    n_slots = n_tok * TOP_K
    nb = n_slots // MOE_TM + N_EXPERTS
    pos_pad, meta = _plan(route)
    block_e, n_used, slot_tok3 = _block_tables(meta, route[:, :TOP_K].astype(jnp.int32), n_tok, nb)
    y = _moe(block_e, n_used, slot_tok3, u2, w_gate[l].astype(BF16),
             w_up[l].astype(BF16), w_down[l].astype(BF16))

    pos3 = pos_pad[:, :TOP_K].reshape(n_tok // COMB_TM, 1, TOP_K * COMB_TM)
    out = _combine(pos3, y, x1, route, gate2, row(ln2_g), row(ln2_b), seq)
    return out.reshape(batch, seq, d)
```

```python
import functools
import math

import jax
import jax.numpy as jnp
from jax import lax
from jax.experimental import pallas as pl
from jax.experimental.pallas import tpu as pltpu

F32 = jnp.float32
BF16 = jnp.bfloat16

D_MODEL = 2048
D_CONV = 1024
D_RET = 1024
CONV_WIDTH = 31
RET_HEADS = 8
HEAD_DIM = 128
RET_CHUNK = 128
ROPE_BASE = 10000.0
N_GROUPS = 4
EXPERTS_PER_GROUP = 8
N_EXPERTS = 32
TOP_K = 2
D_EXPERT = 1024
LN_EPS = 1e-5
DEPTH = 1
DN_ALPHA = (2 * DEPTH) ** 0.25

LANES = 128
SUBLANES = 8
VMEM_LIMIT = 56 * 1024 * 1024

ADA_TN = 1024
INPROJ_TM = 512
INPROJ_TN = 1024
CONV_TT = 512
CONV_HALO = 32
CONV_ROWS = 64
CONV_LANES = 256
RET_TT = 256
OUTPROJ_TM = 256
PLAN_TB = 512
MOE_TM = 256
MOE_FC = 256
MOE_CHUNKS = D_EXPERT // MOE_FC
COMB_TM = 256


def _ln(x):
    mu = jnp.mean(x, axis=-1, keepdims=True)
    xc = x - mu
    var = jnp.mean(xc * xc, axis=-1, keepdims=True)
    return xc * lax.rsqrt(var + LN_EPS)


def _silu(x):
    return x * jax.nn.sigmoid(x)


def _params(sem):
    return pltpu.CompilerParams(dimension_semantics=sem, vmem_limit_bytes=VMEM_LIMIT)


def _ada_kernel(c_ref, w_ref, b_ref, o_ref):
    ca = _silu(c_ref[...]).astype(BF16)
    o_ref[...] = jnp.dot(ca, w_ref[...].astype(BF16), preferred_element_type=F32) + b_ref[...]


def _ada(c_pad, w_ada, b_ada):
    rows, d = c_pad.shape
    n = w_ada.shape[1]
    return pl.pallas_call(
        _ada_kernel,
        grid=(n // ADA_TN,),
        in_specs=[pl.BlockSpec((rows, d), lambda j: (0, 0)),
                  pl.BlockSpec((d, ADA_TN), lambda j: (0, j)),
                  pl.BlockSpec((1, ADA_TN), lambda j: (0, j))],
        out_specs=pl.BlockSpec((rows, ADA_TN), lambda j: (0, j)),
        out_shape=jax.ShapeDtypeStruct((rows, n), F32),
        compiler_params=_params(("arbitrary",)),
        name="ada",
    )(c_pad, w_ada, b_ada)


def _inproj_kernel(x_ref, sc_ref, sh_ref, pos_ref, invf_ref, sgn_ref, w_ref,
                   h0_ref, q_ref, k_ref, v_ref, g_ref,
                   u_scr, a_scr, cos_scr, sin_scr):
    j = pl.program_id(1)

    @pl.when(j == 0)
    def _():
        u = _ln(x_ref[...]) * (1.0 + sc_ref[...]) + sh_ref[...]
        u_scr[...] = u.astype(BF16)
        ang = pos_ref[...].astype(F32) * invf_ref[...]
        cos_scr[...] = jnp.cos(ang)
        sin_scr[...] = jnp.sin(ang) * sgn_ref[...]

    acc = jnp.dot(u_scr[...], w_ref[...], preferred_element_type=F32)

    def rotary_to(out_ref, scale):
        cos = cos_scr[...]
        sin = sin_scr[...]
        for h in range(RET_HEADS):
            xs = acc[:, h * HEAD_DIM:(h + 1) * HEAD_DIM]
            rot = xs * cos + pltpu.roll(xs, HEAD_DIM // 2, axis=1) * sin
            if scale is not None:
                rot = rot * scale
            out_ref[:, h * HEAD_DIM:(h + 1) * HEAD_DIM] = rot.astype(out_ref.dtype)

    @pl.when(j == 0)
    def _():
        a_scr[...] = acc

    @pl.when(j == 1)
    def _():
        h0_ref[...] = a_scr[...] * jax.nn.sigmoid(acc)

    @pl.when(j == 2)
    def _():
        rotary_to(q_ref, None)

    @pl.when(j == 3)
    def _():
        rotary_to(k_ref, HEAD_DIM ** -0.5)

    @pl.when(j == 4)
    def _():
        v_ref[...] = acc.astype(BF16)

    @pl.when(j == 5)
    def _():
        g_ref[...] = _silu(acc)


def _inproj(x2, scale1, shift1, pos2, invf, sgn, w_in_bf, seq):
    n, d = x2.shape
    tm, tn = INPROJ_TM, INPROJ_TN
    tiles_per_seq = seq // tm
    row = lambda i, j: (i, 0)
    per_batch = lambda i, j: (i // tiles_per_seq, 0, 0)
    const = lambda i, j: (0, 0)
    out_shape = [jax.ShapeDtypeStruct((n, D_CONV), F32),
                 jax.ShapeDtypeStruct((n, D_RET), BF16),
                 jax.ShapeDtypeStruct((n, D_RET), BF16),
                 jax.ShapeDtypeStruct((n, D_RET), BF16),
                 jax.ShapeDtypeStruct((n, D_RET), F32)]
    return pl.pallas_call(
        _inproj_kernel,
        grid=(n // tm, w_in_bf.shape[1] // tn),
        in_specs=[pl.BlockSpec((tm, d), row),
                  pl.BlockSpec((None, 1, d), per_batch),
                  pl.BlockSpec((None, 1, d), per_batch),
                  pl.BlockSpec((tm, 1), row),
                  pl.BlockSpec((1, LANES), const),
                  pl.BlockSpec((1, LANES), const),
                  pl.BlockSpec((d, tn), lambda i, j: (0, j))],
        out_specs=[pl.BlockSpec((tm, tn), row) for _ in out_shape],
        out_shape=out_shape,
        scratch_shapes=[pltpu.VMEM((tm, d), BF16),
                        pltpu.VMEM((tm, tn), F32),
                        pltpu.VMEM((tm, LANES), F32),
                        pltpu.VMEM((tm, LANES), F32)],
        compiler_params=_params(("arbitrary", "arbitrary")),
        name="inproj",
    )(x2, scale1, shift1, pos2, invf, sgn, w_in_bf)


def _conv_kernel(main_ref, halo_ref, w_ref, b_ref, g_ref, beta_ref, o_ref, buf, cbuf):
    t = pl.program_id(1)
    buf[0:CONV_HALO, :] = jnp.where(t > 0, halo_ref[...], 0.0)
    buf[CONV_HALO:, :] = main_ref[...]
    first = CONV_HALO - (CONV_WIDTH - 1)
    rows = CONV_ROWS

    def chunk(ci, carry):
        r0 = pl.multiple_of(ci * rows, rows)
        for lg in range(D_CONV // CONV_LANES):
            lanes = slice(lg * CONV_LANES, (lg + 1) * CONV_LANES)
            acc = None
            for s in range(SUBLANES):
                taps = [j for j in range(CONV_WIDTH) if (first + j) % SUBLANES == s]
                span = rows if s == 0 else rows + SUBLANES
                part = None
                for j in taps:
                    a0 = first + j - s
                    term = w_ref[j:j + 1, lanes] * buf[pl.ds(r0 + a0, span), lanes]
                    part = term if part is None else part + term
                part = part[s:s + rows, :]
                acc = part if acc is None else acc + part
            cbuf[pl.ds(r0, rows), lanes] = acc
        return carry

    lax.fori_loop(0, CONV_TT // rows, chunk, 0)
    y = _ln(cbuf[...] + b_ref[...]) * g_ref[...] + beta_ref[...]
    o_ref[...] = _silu(y).astype(BF16)


def _conv(h0, conv_w, conv_b, ln_g, ln_b):
    b, t, c = h0.shape
    tt = CONV_TT
    ratio = tt // CONV_HALO
    const = lambda bi, ti: (0, 0)
    return pl.pallas_call(
        _conv_kernel,
        grid=(b, t // tt),
        in_specs=[pl.BlockSpec((None, tt, c), lambda bi, ti: (bi, ti, 0)),
                  pl.BlockSpec((None, CONV_HALO, c),
                               lambda bi, ti: (bi, jnp.maximum(ti * ratio - 1, 0), 0)),
                  pl.BlockSpec((CONV_WIDTH, c), const),
                  pl.BlockSpec((1, c), const),
                  pl.BlockSpec((1, c), const),
                  pl.BlockSpec((1, c), const)],
        out_specs=pl.BlockSpec((None, tt, c), lambda bi, ti: (bi, ti, 0)),
        out_shape=jax.ShapeDtypeStruct((b, t, c), BF16),
        scratch_shapes=[pltpu.VMEM((tt + CONV_HALO, c), F32),
                        pltpu.VMEM((tt, c), F32)],
        compiler_params=_params(("arbitrary", "arbitrary")),
        name="conv",
    )(h0, h0, conv_w, conv_b, ln_g, ln_b)


def _ret_kernel(q_ref, k_ref, v_ref, gs_ref, dec_ref, xi_ref, zeta_ref, cd_ref, o_ref, st):
    t = pl.program_id(1)

    @pl.when(t == 0)
    def _():
        st[...] = jnp.zeros(st.shape, F32)

    for c in range(RET_TT // RET_CHUNK):
        rows = slice(c * RET_CHUNK, (c + 1) * RET_CHUNK)
        for h in range(RET_HEADS):
            cols = slice(h * HEAD_DIM, (h + 1) * HEAD_DIM)
            q = q_ref[rows, cols]
            k = k_ref[rows, cols]
            v = v_ref[rows, cols]
            s = lax.dot_general(q, k, (((1,), (1,)), ((), ())), preferred_element_type=F32)
            s = s * dec_ref[h]
            inner = jnp.dot(s.astype(BF16), v, preferred_element_type=F32)
            state = st[h]
            qx = (q.astype(F32) * xi_ref[h]).astype(BF16)
            cross = jnp.dot(qx, state.astype(BF16), preferred_element_type=F32)
            kz = (k.astype(F32) * zeta_ref[h]).astype(BF16)
            kv = lax.dot_general(kz, v, (((0,), (0,)), ((), ())), preferred_element_type=F32)
            st[h] = state * cd_ref[h] + kv
            r = _ln(inner + cross)
            o_ref[rows, cols] = (gs_ref[rows, cols] * r).astype(BF16)


def _retention(q, k, v, gs, dec, xi, zeta, cd, batch, seq):
    n, w = q.shape
    tt = RET_TT
    tiles = seq // tt
    row = lambda bi, ti: (bi * tiles + ti, 0)
    const3 = lambda bi, ti: (0, 0, 0)
    tab = pl.BlockSpec((RET_HEADS, RET_CHUNK, HEAD_DIM), const3)
    return pl.pallas_call(
        _ret_kernel,
        grid=(batch, tiles),
        in_specs=[pl.BlockSpec((tt, w), row)] * 4 + [
            tab, tab, tab, pl.BlockSpec((RET_HEADS, 1, HEAD_DIM), const3)],
        out_specs=pl.BlockSpec((tt, w), row),
        out_shape=jax.ShapeDtypeStruct((n, w), BF16),
        scratch_shapes=[pltpu.VMEM((RET_HEADS, HEAD_DIM, HEAD_DIM), F32)],
        compiler_params=_params(("arbitrary", "arbitrary")),
        name="retention",
    )(q, k, v, gs, dec, xi, zeta, cd)


def _outproj_kernel(hc_ref, r_ref, x_ref, w_ref, gate_ref, g1_ref, b1_ref, sc2_ref, sh2_ref,
                    wr_ref, br_ref, x1_ref, u2_ref, route_ref):
    mix = jnp.dot(hc_ref[...], w_ref[0:D_CONV, :], preferred_element_type=F32)
    mix = mix + jnp.dot(r_ref[...], w_ref[D_CONV:, :], preferred_element_type=F32)
    y = DN_ALPHA * x_ref[...] + (1.0 + gate_ref[...]) * mix
    x1 = _ln(y) * g1_ref[...] + b1_ref[...]
    x1_ref[...] = x1
    u2 = _ln(x1) * (1.0 + sc2_ref[...]) + sh2_ref[...]
    u2_ref[...] = u2
    logits = jnp.dot(u2.astype(BF16), wr_ref[...].astype(BF16),
                     preferred_element_type=F32) + br_ref[...]

    lane = lax.broadcasted_iota(jnp.int32, logits.shape, 1).astype(F32)
    neg = jnp.float32(-jnp.inf)
    big = jnp.float32(LANES)
    first_where = lambda m: jnp.min(jnp.where(m, lane, big), axis=-1, keepdims=True)

    is_g = lane < N_GROUPS
    gl = jnp.where(is_g, logits, neg)
    gmax = jnp.max(gl, axis=-1, keepdims=True)
    grp = first_where(gl == gmax)
    gsum = jnp.sum(jnp.where(is_g, jnp.exp(gl - gmax), 0.0), axis=-1, keepdims=True)
    g_w = 1.0 / gsum

    lo = N_GROUPS + grp * EXPERTS_PER_GROUP
    el = jnp.where((lane >= lo) & (lane < lo + EXPERTS_PER_GROUP), logits, neg)
    m1 = jnp.max(el, axis=-1, keepdims=True)
    i1 = first_where(el == m1)
    el2 = jnp.where(lane == i1, neg, el)
    m2 = jnp.max(el2, axis=-1, keepdims=True)
    i2 = first_where(el2 == m2)
    e21 = jnp.exp(m2 - m1)
    w1 = g_w / (1.0 + e21)
    w2 = g_w * e21 / (1.0 + e21)
    route = jnp.where(lane == 0, i1 - N_GROUPS,
                      jnp.where(lane == 1, i2 - N_GROUPS,
                                jnp.where(lane == 2, w1, jnp.where(lane == 3, w2, 0.0))))
    route_ref[...] = route


def _outproj(hc, r, x2, w_out_bf, gate1, ln1_g, ln1_b, scale2, shift2, w_router, b_router, seq):
    n, d = x2.shape
    tm = OUTPROJ_TM
    tiles_per_seq = seq // tm
    row = lambda i: (i, 0)
    per_batch = lambda i: (i // tiles_per_seq, 0, 0)
    const = lambda i: (0, 0)
    vec = pl.BlockSpec((1, d), const)
    mod = pl.BlockSpec((None, 1, d), per_batch)
    return pl.pallas_call(
        _outproj_kernel,
        grid=(n // tm,),
        in_specs=[pl.BlockSpec((tm, D_CONV), row),
                  pl.BlockSpec((tm, D_RET), row),
                  pl.BlockSpec((tm, d), row),
                  pl.BlockSpec((d, d), const),
                  mod, vec, vec, mod, mod,
                  pl.BlockSpec((d, LANES), const),
                  pl.BlockSpec((1, LANES), const)],
        out_specs=[pl.BlockSpec((tm, d), row),
                   pl.BlockSpec((tm, d), row),
                   pl.BlockSpec((tm, LANES), row)],
        out_shape=[jax.ShapeDtypeStruct((n, d), F32),
                   jax.ShapeDtypeStruct((n, d), F32),
                   jax.ShapeDtypeStruct((n, LANES), F32)],
        compiler_params=_params(("arbitrary",)),
        name="outproj",
    )(hc, r, x2, w_out_bf, gate1, ln1_g, ln1_b, scale2, shift2, w_router, b_router)


def _plan_kernel(route_ref, pos_ref, meta_ref, tri, upper, run, pstart):
    ph = pl.program_id(0)
    i = pl.program_id(1)
    tb = PLAN_TB
    lane = lax.broadcasted_iota(jnp.int32, (tb, LANES), 1).astype(F32)
    e1 = route_ref[:, 0:1]
    e2 = route_ref[:, 1:2]
    member = (lane == e1) | (lane == e2)
    col_count = jnp.sum(member.astype(F32), axis=0, keepdims=True)

    @pl.when((ph == 0) & (i == 0))
    def _():
        r = lax.broadcasted_iota(jnp.int32, (tb, tb), 0)
        c = lax.broadcasted_iota(jnp.int32, (tb, tb), 1)
        tri[...] = (c < r).astype(BF16)
        ru = lax.broadcasted_iota(jnp.int32, (LANES, LANES), 0)
        cu = lax.broadcasted_iota(jnp.int32, (LANES, LANES), 1)
        upper[...] = (ru < cu).astype(BF16)
        run[...] = jnp.zeros(run.shape, F32)

    @pl.when(ph == 0)
    def _():
        run[...] += col_count

    @pl.when((ph == 0) & (i == pl.num_programs(1) - 1))
    def _():
        counts = run[...]
        blocks = jnp.floor((counts + (MOE_TM - 1)) * (1.0 / MOE_TM))
        start_blocks = jnp.dot(jnp.broadcast_to(blocks, (SUBLANES, LANES)).astype(BF16), upper[...],
                               preferred_element_type=F32)[0:1, :]
        pstart[...] = start_blocks * MOE_TM
        row = lax.broadcasted_iota(jnp.int32, (SUBLANES, LANES), 0)
        meta_ref[...] = jnp.where(row == 0, counts, jnp.where(row == 1, blocks,
                                  jnp.where(row == 2, start_blocks, 0.0)))
        run[...] = jnp.zeros(run.shape, F32)

    @pl.when(ph == 1)
    def _():
        earlier = jnp.dot(tri[...], member.astype(BF16), preferred_element_type=F32) + run[...]
        dest = earlier + pstart[...]
        p1 = jnp.sum(jnp.where(lane == e1, dest, 0.0), axis=-1, keepdims=True)
        p2 = jnp.sum(jnp.where(lane == e2, dest, 0.0), axis=-1, keepdims=True)
        pos_ref[...] = jnp.where(lane == 0, p1, jnp.where(lane == 1, p2, 0.0)).astype(jnp.int32)
        run[...] += col_count


def _plan(route):
    n = route.shape[0]
    tb = PLAN_TB
    return pl.pallas_call(
        _plan_kernel,
        grid=(2, n // tb),
        in_specs=[pl.BlockSpec((tb, LANES), lambda ph, i: (i, 0))],
        out_specs=[pl.BlockSpec((tb, LANES), lambda ph, i: (ph * i, 0)),
                   pl.BlockSpec((SUBLANES, LANES), lambda ph, i: (0, 0))],
        out_shape=[jax.ShapeDtypeStruct((n, LANES), jnp.int32),
                   jax.ShapeDtypeStruct((SUBLANES, LANES), F32)],
        scratch_shapes=[pltpu.VMEM((tb, tb), BF16),
                        pltpu.VMEM((LANES, LANES), BF16),
                        pltpu.VMEM((1, LANES), F32),
                        pltpu.VMEM((1, LANES), F32)],
        compiler_params=_params(("arbitrary", "arbitrary")),
        name="plan",
    )(route)


MOE_COMPUTE, MOE_CAST, MOE_ZERO = 1, 2, 4
MOE_PAR_COMPUTE_SHIFT, MOE_PAR_CAST_SHIFT = 3, 4


def _moe_kernel(flag_ref, rb_ref, fe_ref, fc_ref, nu_ref, idx_ref, idx_next_ref, u2_hbm,
                wg_ref, wu_ref, wd_ref, y_ref, wgb, wub, wdb, xbuf, sem):
    s = pl.program_id(0)
    flags = flag_ref[s]
    n_used = nu_ref[0]
    rb = rb_ref[s]
    slot = rb % 2

    def start_gather(rows_ref, dst):
        for r in range(MOE_TM):
            pltpu.make_async_copy(u2_hbm.at[pl.ds(rows_ref[0, 0, r], 1)],
                                  xbuf.at[dst, pl.ds(r, 1)], sem.at[dst]).start()

    @pl.when(s == 0)
    def _():
        start_gather(idx_ref, 0)

    @pl.when((flags & MOE_CAST) != 0)
    def _():
        par = (flags >> MOE_PAR_CAST_SHIFT) & 1
        c = fc_ref[s]
        wgb[par, c] = wg_ref[0].astype(BF16)
        wub[par, c] = wu_ref[0].astype(BF16)
        wdb[par, c] = wd_ref[0].astype(BF16)

    @pl.when((flags & MOE_COMPUTE) != 0)
    def _():
        pltpu.make_async_copy(u2_hbm.at[pl.ds(0, MOE_TM)], xbuf.at[slot], sem.at[slot]).wait()

        @pl.when(rb + 1 < n_used)
        def _():
            start_gather(idx_next_ref, 1 - slot)

        par = (flags >> MOE_PAR_COMPUTE_SHIFT) & 1
        x = xbuf[slot].astype(BF16)
        acc = jnp.zeros((MOE_TM, D_MODEL), F32)
        for c in range(MOE_CHUNKS):
            hg = jnp.dot(x, wgb[par, c], preferred_element_type=F32)
            hu = jnp.dot(x, wub[par, c], preferred_element_type=F32)
            hid = (_silu(hg) * hu).astype(BF16)
            acc = acc + jnp.dot(hid, wdb[par, c], preferred_element_type=F32)
        y_ref[...] = acc

    @pl.when((flags & MOE_ZERO) != 0)
    def _():
        y_ref[...] = jnp.zeros(y_ref.shape, F32)


def _moe(steps, slot_tok3, u2, w_gate, w_up, w_down):
    flags, rb, fetch_e, fetch_c, n_used = steps
    nb = slot_tok3.shape[0]
    d = u2.shape[1]
    tm, fc = MOE_TM, MOE_FC
    smem_idx = lambda fn: pl.BlockSpec((1, 1, tm), fn, memory_space=pltpu.SMEM)
    grid_spec = pltpu.PrefetchScalarGridSpec(
        num_scalar_prefetch=5,
        grid=(flags.shape[0],),
        in_specs=[smem_idx(lambda s, fl, rb, fe, fc_, nu: (rb[s], 0, 0)),
                  smem_idx(lambda s, fl, rb, fe, fc_, nu: (jnp.minimum(rb[s] + 1, nb - 1), 0, 0)),
                  pl.BlockSpec(memory_space=pl.ANY),
                  pl.BlockSpec((1, d, fc), lambda s, fl, rb, fe, fc_, nu: (fe[s], 0, fc_[s])),
                  pl.BlockSpec((1, d, fc), lambda s, fl, rb, fe, fc_, nu: (fe[s], 0, fc_[s])),
                  pl.BlockSpec((1, fc, d), lambda s, fl, rb, fe, fc_, nu: (fe[s], fc_[s], 0))],
        out_specs=pl.BlockSpec((tm, d), lambda s, fl, rb, fe, fc_, nu: (rb[s], 0)),
        scratch_shapes=[pltpu.VMEM((2, MOE_CHUNKS, d, fc), BF16),
                        pltpu.VMEM((2, MOE_CHUNKS, d, fc), BF16),
                        pltpu.VMEM((2, MOE_CHUNKS, fc, d), BF16),
                        pltpu.VMEM((2, tm, d), F32),
                        pltpu.SemaphoreType.DMA((2,))],
    )
    return pl.pallas_call(
        _moe_kernel,
        grid_spec=grid_spec,
        out_shape=jax.ShapeDtypeStruct((nb * tm, d), F32),
        compiler_params=_params(("arbitrary",)),
        name="moe",
    )(flags, rb, fetch_e, fetch_c, n_used, slot_tok3, slot_tok3, u2, w_gate, w_up, w_down)


def _combine_kernel(pos_ref, pos_next_ref, y_hbm, x1_ref, route_ref, gate_ref, g_ref, b_ref,
                    o_ref, ybuf, sem):
    i = pl.program_id(0)
    n_steps = pl.num_programs(0)
    slot = i % 2

    def start_gather(rows_ref, s):
        for r in range(COMB_TM):
            for kk in range(TOP_K):
                pltpu.make_async_copy(y_hbm.at[pl.ds(rows_ref[0, 0, TOP_K * r + kk], 1)],
                                      ybuf.at[s, kk, pl.ds(r, 1)], sem.at[s]).start()

    @pl.when(i == 0)
    def _():
        start_gather(pos_ref, 0)

    for kk in range(TOP_K):
        pltpu.make_async_copy(y_hbm.at[pl.ds(0, COMB_TM)], ybuf.at[slot, kk], sem.at[slot]).wait()

    @pl.when(i + 1 < n_steps)
    def _():
        start_gather(pos_next_ref, 1 - slot)

    w1 = route_ref[:, 2:3]
    w2 = route_ref[:, 3:4]
    ffn = w1 * ybuf[slot, 0] + w2 * ybuf[slot, 1]
    y = DN_ALPHA * x1_ref[...] + (1.0 + gate_ref[...]) * ffn
    o_ref[...] = _ln(y) * g_ref[...] + b_ref[...]


def _combine(pos3, y, x1, route, gate2, ln2_g, ln2_b, seq):
    n, d = x1.shape
    tm = COMB_TM
    steps = n // tm
    tiles_per_seq = seq // tm
    row = lambda i: (i, 0)
    const = lambda i: (0, 0)
    smem_idx = lambda fn: pl.BlockSpec((1, 1, TOP_K * tm), fn, memory_space=pltpu.SMEM)
    return pl.pallas_call(
        _combine_kernel,
        grid=(steps,),
        in_specs=[smem_idx(lambda i: (i, 0, 0)),
                  smem_idx(lambda i: (jnp.minimum(i + 1, steps - 1), 0, 0)),
                  pl.BlockSpec(memory_space=pl.ANY),
                  pl.BlockSpec((tm, d), row),
                  pl.BlockSpec((tm, LANES), row),
                  pl.BlockSpec((None, 1, d), lambda i: (i // tiles_per_seq, 0, 0)),
                  pl.BlockSpec((1, d), const),
                  pl.BlockSpec((1, d), const)],
        out_specs=pl.BlockSpec((tm, d), row),
        out_shape=jax.ShapeDtypeStruct((n, d), F32),
        scratch_shapes=[pltpu.VMEM((2, TOP_K, tm, d), F32),
                        pltpu.SemaphoreType.DMA((2,))],
        compiler_params=_params(("arbitrary",)),
        name="combine",
    )(pos3, pos3, y, x1, route, gate2, ln2_g, ln2_b)


def _retention_tables():
    h = jnp.arange(RET_HEADS, dtype=F32)
    log_gamma = jnp.log1p(-jnp.exp2(-5.0 - h))
    idx = jnp.arange(RET_CHUNK, dtype=F32)
    diff = idx[:, None] - idx[None, :]
    causal = diff >= 0
    dec = jnp.where(causal[None],
                    jnp.exp(jnp.where(causal, diff, 0.0)[None] * log_gamma[:, None, None]), 0.0)
    xi = jnp.exp((idx[None, :] + 1.0) * log_gamma[:, None])
    zeta = jnp.exp((RET_CHUNK - 1.0 - idx[None, :]) * log_gamma[:, None])
    cd = jnp.exp(RET_CHUNK * log_gamma)
    bc = lambda a: jnp.broadcast_to(a[:, :, None], (RET_HEADS, RET_CHUNK, HEAD_DIM))
    return dec, bc(xi), bc(zeta), jnp.broadcast_to(cd[:, None, None], (RET_HEADS, 1, HEAD_DIM))


def _rope_tables():
    half = HEAD_DIM // 2
    inv_freq = jnp.exp(-math.log(ROPE_BASE) * jnp.arange(half, dtype=F32) / half)
    invf = jnp.concatenate([inv_freq, inv_freq])[None, :]
    sgn = jnp.concatenate([-jnp.ones((half,), F32), jnp.ones((half,), F32)])[None, :]
    return invf, sgn


def _step_tables(meta, nb):
    i32 = jnp.int32
    blocks = meta[1, :N_EXPERTS].astype(i32)
    first_block = meta[2, :N_EXPERTS].astype(i32)
    e_iota = jnp.arange(N_EXPERTS, dtype=i32)
    present = blocks > 0
    n_used = jnp.sum(blocks)
    seg_len = jnp.where(present, jnp.maximum(blocks, MOE_CHUNKS), 0)
    seg_end = MOE_CHUNKS + jnp.cumsum(seg_len)
    seg_start = seg_end - seg_len
    s_used = seg_end[-1]
    ordinal = jnp.cumsum(present.astype(i32)) - present.astype(i32)
    later = present[None, :] & (e_iota[None, :] > e_iota[:, None])
    next_e = jnp.min(jnp.where(later, e_iota[None, :], N_EXPERTS), axis=1)
    first_e = jnp.min(jnp.where(present, e_iota, N_EXPERTS))
    last_e = jnp.max(jnp.where(present, e_iota, -1))

    n_steps = MOE_CHUNKS + nb + (MOE_CHUNKS - 1) * (N_EXPERTS - 1)
    s = jnp.arange(n_steps, dtype=i32)
    prologue = s < MOE_CHUNKS
    in_seg = (~prologue) & (s < s_used)
    e_s = jnp.minimum(jnp.sum((seg_end[None, :] <= s[:, None]).astype(i32), axis=1), N_EXPERTS - 1)
    onehot = (e_s[:, None] == e_iota[None, :]).astype(i32)
    pick = lambda t: jnp.sum(onehot * t[None, :], axis=1)
    k = s - pick(seg_start)
    nblk = pick(blocks)
    has_next = pick(next_e) < N_EXPERTS
    compute = in_seg & (k < nblk)
    rb_seg = pick(first_block) + jnp.minimum(k, nblk)
    rb_tail = n_used + (s - s_used)
    rb = jnp.where(prologue, 0, jnp.where(in_seg, rb_seg, rb_tail))
    zero = (s >= s_used) & (rb_tail < nb)
    rb = jnp.clip(rb, 0, nb - 1)
    cast = prologue | (in_seg & has_next & (k < MOE_CHUNKS))
    fetch_e = jnp.where(prologue, first_e,
                        jnp.where(in_seg, jnp.where(has_next, pick(next_e), e_s), last_e))
    fetch_c = jnp.where(prologue, s, jnp.where(in_seg & has_next,
                                               jnp.minimum(k, MOE_CHUNKS - 1), MOE_CHUNKS - 1))
    par_compute = pick(ordinal) % 2
    par_cast = jnp.where(prologue, 0, (pick(ordinal) + 1) % 2)
    flags = (compute.astype(i32) * MOE_COMPUTE + cast.astype(i32) * MOE_CAST
             + zero.astype(i32) * MOE_ZERO + (par_compute << MOE_PAR_COMPUTE_SHIFT)
             + (par_cast << MOE_PAR_CAST_SHIFT))
    return flags, rb, fetch_e, fetch_c, n_used.reshape(1)


def _row_tokens(meta, expert_ids, n_tok, nb):
    counts = meta[0, :N_EXPERTS].astype(jnp.int32)
    blocks = meta[1, :N_EXPERTS].astype(jnp.int32)
    n_slots = n_tok * TOP_K
    fill = blocks * MOE_TM - counts
    e_iota = jnp.arange(N_EXPERTS, dtype=jnp.int32)[:, None]
    r_iota = jnp.arange(MOE_TM, dtype=jnp.int32)[None, :]
    filler_keys = jnp.where(r_iota < fill[:, None], e_iota, N_EXPERTS).reshape(-1)
    order = jnp.argsort(jnp.concatenate([expert_ids.reshape(-1), filler_keys])).astype(jnp.int32)
    slot_tok = jnp.where(order < n_slots, order // TOP_K, 0)
    return slot_tok.reshape(nb, 1, MOE_TM)


def kernel(x, c, positions, w_ada, b_ada, w_in, conv_w, conv_b, conv_ln_g, conv_ln_b, w_out,
           ln1_g, ln1_b, w_group_router, b_group_router, w_expert_router, b_expert_router,
           w_gate, w_up, w_down, ln2_g, ln2_b):
    batch, seq, d = x.shape
    n_tok = batch * seq
    l = 0
    row = lambda a: a[l][None, :]

    c_pad = jnp.zeros((8, d), F32).at[:batch].set(c)
    mod = _ada(c_pad, w_ada[l], b_ada[l][None, :])[:batch]
    shift1, scale1, gate1, shift2, scale2, gate2 = [m[:, None, :] for m in jnp.split(mod, 6, axis=-1)]

    x2 = x.reshape(n_tok, d)
    invf, sgn = _rope_tables()
    h0, q, k, v, gs = _inproj(x2, scale1, shift1, positions.reshape(n_tok, 1), invf, sgn,
                              w_in[l].astype(BF16), seq)

    hc = _conv(h0.reshape(batch, seq, D_CONV), conv_w[l], row(conv_b), row(conv_ln_g),
               row(conv_ln_b)).reshape(n_tok, D_CONV)

    dec, xi, zeta, cd = _retention_tables()
    r = _retention(q, k, v, gs, dec, xi, zeta, cd, batch, seq)

    n_route = N_GROUPS + N_EXPERTS
    w_router = jnp.zeros((d, LANES), F32).at[:, :N_GROUPS].set(w_group_router[l])
    w_router = w_router.at[:, N_GROUPS:n_route].set(w_expert_router[l])
    b_router = jnp.zeros((1, LANES), F32).at[0, :N_GROUPS].set(b_group_router[l])
    b_router = b_router.at[0, N_GROUPS:n_route].set(b_expert_router[l])
    x1, u2, route = _outproj(hc, r, x2, w_out[l].astype(BF16), gate1, row(ln1_g), row(ln1_b),
                             scale2, shift2, w_router, b_router, seq)

    n_slots = n_tok * TOP_K
    nb = n_slots // MOE_TM + N_EXPERTS
    assert N_EXPERTS * (MOE_CHUNKS - 1) * MOE_TM < n_slots
    pos_pad, meta = _plan(route)
    slot_tok3 = _row_tokens(meta, route[:, :TOP_K].astype(jnp.int32), n_tok, nb)
    y = _moe(_step_tables(meta, nb), slot_tok3, u2, w_gate[l], w_up[l], w_down[l])

    pos3 = pos_pad[:, :TOP_K].reshape(n_tok // COMB_TM, 1, TOP_K * COMB_TM)
    out = _combine(pos3, y, x1, route, gate2, row(ln2_g), row(ln2_b), seq)
    return out.reshape(batch, seq, d)
```

```python
import functools
import math

import jax
import jax.numpy as jnp
from jax import lax
from jax.experimental import pallas as pl
from jax.experimental.pallas import tpu as pltpu

F32 = jnp.float32
BF16 = jnp.bfloat16

D_MODEL = 2048
D_CONV = 1024
D_RET = 1024
CONV_WIDTH = 31
RET_HEADS = 8
HEAD_DIM = 128
RET_CHUNK = 128
ROPE_BASE = 10000.0
N_GROUPS = 4
EXPERTS_PER_GROUP = 8
N_EXPERTS = 32
TOP_K = 2
D_EXPERT = 1024
LN_EPS = 1e-5
DEPTH = 1
DN_ALPHA = (2 * DEPTH) ** 0.25

LANES = 128
SUBLANES = 8
TOK_ROWS = D_MODEL // LANES
TOK_PITCH = TOK_ROWS + SUBLANES
VMEM_LIMIT = 56 * 1024 * 1024

ADA_TN = 1024
INPROJ_TM = 512
INPROJ_TN = 1024
CONV_TT = 512
CONV_HALO = 32
CONV_ROWS = 64
CONV_LANES = 256
RET_TT = 256
OUTPROJ_TM = 256
PLAN_TB = 512
MOE_TM = 256
MOE_FC = 256
MOE_CHUNKS = D_EXPERT // MOE_FC
COMB_TM = 256


def _ln(x):
    mu = jnp.mean(x, axis=-1, keepdims=True)
    xc = x - mu
    var = jnp.mean(xc * xc, axis=-1, keepdims=True)
    return xc * lax.rsqrt(var + LN_EPS)


def _silu(x):
    return x * jax.nn.sigmoid(x)


def _store_token_major(ref, val):
    rows = val.shape[0]
    for s in range(TOK_ROWS):
        ref[pl.ds(s, rows, stride=TOK_ROWS), :] = val[:, s * LANES:(s + 1) * LANES]


def _load_token_major(ref, rows):
    return jnp.concatenate([ref[pl.ds(s, rows, stride=TOK_PITCH), :] for s in range(TOK_ROWS)],
                           axis=1)


def _params(sem):
    return pltpu.CompilerParams(dimension_semantics=sem, vmem_limit_bytes=VMEM_LIMIT)


def _ada_kernel(c_ref, w_ref, b_ref, o_ref):
    ca = _silu(c_ref[...]).astype(BF16)
    o_ref[...] = jnp.dot(ca, w_ref[...].astype(BF16), preferred_element_type=F32) + b_ref[...]


def _ada(c_pad, w_ada, b_ada):
    rows, d = c_pad.shape
    n = w_ada.shape[1]
    return pl.pallas_call(
        _ada_kernel,
        grid=(n // ADA_TN,),
        in_specs=[pl.BlockSpec((rows, d), lambda j: (0, 0)),
                  pl.BlockSpec((d, ADA_TN), lambda j: (0, j)),
                  pl.BlockSpec((1, ADA_TN), lambda j: (0, j))],
        out_specs=pl.BlockSpec((rows, ADA_TN), lambda j: (0, j)),
        out_shape=jax.ShapeDtypeStruct((rows, n), F32),
        compiler_params=_params(("arbitrary",)),
        name="ada",
    )(c_pad, w_ada, b_ada)


def _inproj_kernel(x_ref, sc_ref, sh_ref, pos_ref, invf_ref, sgn_ref, w_ref,
                   h0_ref, q_ref, k_ref, v_ref, g_ref,
                   u_scr, cos_scr, sin_scr, acc0, acc1, acc2):
    i = pl.program_id(0)
    j = pl.program_id(1)
    cur = i % 2
    prev = 1 - cur

    def normalise(dst, rows):
        u = _ln(x_ref[rows, :]) * (1.0 + sc_ref[...]) + sh_ref[...]
        u_scr[dst, rows, :] = u.astype(BF16)
        ang = pos_ref[rows, :].astype(F32) * invf_ref[...]
        cos_scr[dst, rows, :] = jnp.cos(ang)
        sin_scr[dst, rows, :] = jnp.sin(ang) * sgn_ref[...]

    def matmul(src=prev, cols=slice(None)):
        return jnp.dot(u_scr[src], w_ref[:, cols], preferred_element_type=F32)

    def rotary_to(out_ref, src, scale):
        cos = cos_scr[prev]
        sin = sin_scr[prev]
        for h in range(RET_HEADS):
            xs = src[:, h * HEAD_DIM:(h + 1) * HEAD_DIM]
            rot = xs * cos + pltpu.roll(xs, HEAD_DIM // 2, axis=1) * sin
            if scale is not None:
                rot = rot * scale
            out_ref[:, h * HEAD_DIM:(h + 1) * HEAD_DIM] = rot.astype(out_ref.dtype)

    tm = x_ref.shape[0]

    @pl.when((j == 0) & (i == 0))
    def _():
        normalise(0, slice(0, tm))

    quarters = 4
    for parity in range(2):
        @pl.when((j == 0) & (i > 0) & (cur == parity))
        def _(parity=parity):
            for qt in range(quarters):
                cols = slice(qt * (INPROJ_TN // quarters), (qt + 1) * (INPROJ_TN // quarters))
                acc0[:, cols] = matmul(1 - parity, cols)
                normalise(parity, slice(qt * (tm // quarters), (qt + 1) * (tm // quarters)))

    @pl.when((j == 1) & (i > 0))
    def _():
        acc1[...] = matmul()

    @pl.when((j == 2) & (i > 0))
    def _():
        acc2[...] = matmul()
        h0_ref[...] = acc0[...] * jax.nn.sigmoid(acc1[...])

    @pl.when((j == 3) & (i > 0))
    def _():
        acc0[...] = matmul()
        rotary_to(q_ref, acc2, None)

    @pl.when((j == 4) & (i > 0))
    def _():
        acc1[...] = matmul()
        rotary_to(k_ref, acc0, HEAD_DIM ** -0.5)

    @pl.when((j == 5) & (i > 0))
    def _():
        res = matmul()
        v_ref[...] = acc1[...].astype(BF16)
        g_ref[...] = _silu(res)


def _inproj(x2, scale1, shift1, pos2, invf, sgn, w_in_bf, seq):
    n, d = x2.shape
    tm, tn = INPROJ_TM, INPROJ_TN
    tiles_per_seq = seq // tm
    n_tiles = n // tm
    assert w_in_bf.shape[1] == 6 * tn
    norm_tile = lambda i: jnp.minimum(i, n_tiles - 1)
    row_in = lambda i, j: (norm_tile(i), 0)
    per_batch = lambda i, j: (norm_tile(i) // tiles_per_seq, 0, 0)
    row_out = lambda i, j: (jnp.maximum(i - 1, 0), 0)
    const = lambda i, j: (0, 0)
    out_shape = [jax.ShapeDtypeStruct((n, D_CONV), F32),
                 jax.ShapeDtypeStruct((n, D_RET), BF16),
                 jax.ShapeDtypeStruct((n, D_RET), BF16),
                 jax.ShapeDtypeStruct((n, D_RET), BF16),
                 jax.ShapeDtypeStruct((n, D_RET), F32)]
    return pl.pallas_call(
        _inproj_kernel,
        grid=(n_tiles + 1, 6),
        in_specs=[pl.BlockSpec((tm, d), row_in),
                  pl.BlockSpec((None, 1, d), per_batch),
                  pl.BlockSpec((None, 1, d), per_batch),
                  pl.BlockSpec((tm, 1), row_in),
                  pl.BlockSpec((1, LANES), const),
                  pl.BlockSpec((1, LANES), const),
                  pl.BlockSpec((d, tn), lambda i, j: (0, jnp.where(i > 0, j, 0)))],
        out_specs=[pl.BlockSpec((tm, tn), row_out) for _ in out_shape],
        out_shape=out_shape,
        scratch_shapes=[pltpu.VMEM((2, tm, d), BF16),
                        pltpu.VMEM((2, tm, LANES), F32),
                        pltpu.VMEM((2, tm, LANES), F32),
                        pltpu.VMEM((tm, tn), F32),
                        pltpu.VMEM((tm, tn), F32),
                        pltpu.VMEM((tm, tn), F32)],
        compiler_params=_params(("arbitrary", "arbitrary")),
        name="inproj",
    )(x2, scale1, shift1, pos2, invf, sgn, w_in_bf)


def _conv_kernel(main_ref, halo_ref, w_ref, b_ref, g_ref, beta_ref, o_ref, buf, cbuf):
    t = pl.program_id(1)
    buf[0:CONV_HALO, :] = jnp.where(t > 0, halo_ref[...], 0.0)
    buf[CONV_HALO:, :] = main_ref[...]
    first = CONV_HALO - (CONV_WIDTH - 1)
    rows = CONV_ROWS

    def chunk(ci, carry):
        r0 = pl.multiple_of(ci * rows, rows)
        for lg in range(D_CONV // CONV_LANES):
            lanes = slice(lg * CONV_LANES, (lg + 1) * CONV_LANES)
            acc = None
            for s in range(SUBLANES):
                taps = [j for j in range(CONV_WIDTH) if (first + j) % SUBLANES == s]
                span = rows if s == 0 else rows + SUBLANES
                part = None
                for j in taps:
                    a0 = first + j - s
                    term = w_ref[j:j + 1, lanes] * buf[pl.ds(r0 + a0, span), lanes]
                    part = term if part is None else part + term
                part = part[s:s + rows, :]
                acc = part if acc is None else acc + part
            cbuf[pl.ds(r0, rows), lanes] = acc
        return carry

    lax.fori_loop(0, CONV_TT // rows, chunk, 0)
    y = _ln(cbuf[...] + b_ref[...]) * g_ref[...] + beta_ref[...]
    o_ref[...] = _silu(y).astype(BF16)


def _conv(h0, conv_w, conv_b, ln_g, ln_b):
    b, t, c = h0.shape
    tt = CONV_TT
    ratio = tt // CONV_HALO
    const = lambda bi, ti: (0, 0)
    return pl.pallas_call(
        _conv_kernel,
        grid=(b, t // tt),
        in_specs=[pl.BlockSpec((None, tt, c), lambda bi, ti: (bi, ti, 0)),
                  pl.BlockSpec((None, CONV_HALO, c),
                               lambda bi, ti: (bi, jnp.maximum(ti * ratio - 1, 0), 0)),
                  pl.BlockSpec((CONV_WIDTH, c), const),
                  pl.BlockSpec((1, c), const),
                  pl.BlockSpec((1, c), const),
                  pl.BlockSpec((1, c), const)],
        out_specs=pl.BlockSpec((None, tt, c), lambda bi, ti: (bi, ti, 0)),
        out_shape=jax.ShapeDtypeStruct((b, t, c), BF16),
        scratch_shapes=[pltpu.VMEM((tt + CONV_HALO, c), F32),
                        pltpu.VMEM((tt, c), F32)],
        compiler_params=_params(("arbitrary", "arbitrary")),
        name="conv",
    )(h0, h0, conv_w, conv_b, ln_g, ln_b)


def _ret_kernel(q_ref, k_ref, v_ref, gs_ref, dec_ref, xi_ref, zeta_ref, cd_ref, o_ref, st):
    t = pl.program_id(1)

    @pl.when(t == 0)
    def _():
        st[...] = jnp.zeros(st.shape, F32)

    for c in range(RET_TT // RET_CHUNK):
        rows = slice(c * RET_CHUNK, (c + 1) * RET_CHUNK)
        for h in range(RET_HEADS):
            cols = slice(h * HEAD_DIM, (h + 1) * HEAD_DIM)
            q = q_ref[rows, cols]
            k = k_ref[rows, cols]
            v = v_ref[rows, cols]
            s = lax.dot_general(q, k, (((1,), (1,)), ((), ())), preferred_element_type=F32)
            s = s * dec_ref[h]
            inner = jnp.dot(s.astype(BF16), v, preferred_element_type=F32)
            state = st[h]
            qx = (q.astype(F32) * xi_ref[h]).astype(BF16)
            cross = jnp.dot(qx, state.astype(BF16), preferred_element_type=F32)
            kz = (k.astype(F32) * zeta_ref[h]).astype(BF16)
            kv = lax.dot_general(kz, v, (((0,), (0,)), ((), ())), preferred_element_type=F32)
            st[h] = state * cd_ref[h] + kv
            r = _ln(inner + cross)
            o_ref[rows, cols] = (gs_ref[rows, cols] * r).astype(BF16)


def _retention(q, k, v, gs, dec, xi, zeta, cd, batch, seq):
    n, w = q.shape
    tt = RET_TT
    tiles = seq // tt
    row = lambda bi, ti: (bi * tiles + ti, 0)
    const3 = lambda bi, ti: (0, 0, 0)
    tab = pl.BlockSpec((RET_HEADS, RET_CHUNK, HEAD_DIM), const3)
    return pl.pallas_call(
        _ret_kernel,
        grid=(batch, tiles),
        in_specs=[pl.BlockSpec((tt, w), row)] * 4 + [
            tab, tab, tab, pl.BlockSpec((RET_HEADS, 1, HEAD_DIM), const3)],
        out_specs=pl.BlockSpec((tt, w), row),
        out_shape=jax.ShapeDtypeStruct((n, w), BF16),
        scratch_shapes=[pltpu.VMEM((RET_HEADS, HEAD_DIM, HEAD_DIM), F32)],
        compiler_params=_params(("arbitrary", "arbitrary")),
        name="retention",
    )(q, k, v, gs, dec, xi, zeta, cd)


def _outproj_kernel(hc_ref, r_ref, x_ref, w_ref, gate_ref, g1_ref, b1_ref, sc2_ref, sh2_ref,
                    wr_ref, br_ref, x1_ref, u2_ref, route_ref):
    mix = jnp.dot(hc_ref[...], w_ref[0:D_CONV, :], preferred_element_type=F32)
    mix = mix + jnp.dot(r_ref[...], w_ref[D_CONV:, :], preferred_element_type=F32)
    y = DN_ALPHA * x_ref[...] + (1.0 + gate_ref[...]) * mix
    x1 = _ln(y) * g1_ref[...] + b1_ref[...]
    x1_ref[...] = x1
    u2 = _ln(x1) * (1.0 + sc2_ref[...]) + sh2_ref[...]
    _store_token_major(u2_ref, u2)
    logits = jnp.dot(u2.astype(BF16), wr_ref[...].astype(BF16),
                     preferred_element_type=F32) + br_ref[...]

    lane = lax.broadcasted_iota(jnp.int32, logits.shape, 1).astype(F32)
    neg = jnp.float32(-jnp.inf)
    big = jnp.float32(LANES)
    first_where = lambda m: jnp.min(jnp.where(m, lane, big), axis=-1, keepdims=True)

    is_g = lane < N_GROUPS
    gl = jnp.where(is_g, logits, neg)
    gmax = jnp.max(gl, axis=-1, keepdims=True)
    grp = first_where(gl == gmax)
    gsum = jnp.sum(jnp.where(is_g, jnp.exp(gl - gmax), 0.0), axis=-1, keepdims=True)
    g_w = 1.0 / gsum

    lo = N_GROUPS + grp * EXPERTS_PER_GROUP
    el = jnp.where((lane >= lo) & (lane < lo + EXPERTS_PER_GROUP), logits, neg)
    m1 = jnp.max(el, axis=-1, keepdims=True)
    i1 = first_where(el == m1)
    el2 = jnp.where(lane == i1, neg, el)
    m2 = jnp.max(el2, axis=-1, keepdims=True)
    i2 = first_where(el2 == m2)
    e21 = jnp.exp(m2 - m1)
    w1 = g_w / (1.0 + e21)
    w2 = g_w * e21 / (1.0 + e21)
    route = jnp.where(lane == 0, i1 - N_GROUPS,
                      jnp.where(lane == 1, i2 - N_GROUPS,
                                jnp.where(lane == 2, w1, jnp.where(lane == 3, w2, 0.0))))
    route_ref[...] = route


def _outproj(hc, r, x2, w_out_bf, gate1, ln1_g, ln1_b, scale2, shift2, w_router, b_router, seq):
    n, d = x2.shape
    tm = OUTPROJ_TM
    tiles_per_seq = seq // tm
    row = lambda i: (i, 0)
    per_batch = lambda i: (i // tiles_per_seq, 0, 0)
    const = lambda i: (0, 0)
    vec = pl.BlockSpec((1, d), const)
    mod = pl.BlockSpec((None, 1, d), per_batch)
    return pl.pallas_call(
        _outproj_kernel,
        grid=(n // tm,),
        in_specs=[pl.BlockSpec((tm, D_CONV), row),
                  pl.BlockSpec((tm, D_RET), row),
                  pl.BlockSpec((tm, d), row),
                  pl.BlockSpec((d, d), const),
                  mod, vec, vec, mod, mod,
                  pl.BlockSpec((d, LANES), const),
                  pl.BlockSpec((1, LANES), const)],
        out_specs=[pl.BlockSpec((tm, d), row),
                   pl.BlockSpec((tm * TOK_ROWS, LANES), row),
                   pl.BlockSpec((tm, LANES), row)],
        out_shape=[jax.ShapeDtypeStruct((n, d), F32),
                   jax.ShapeDtypeStruct((n * TOK_ROWS, LANES), F32),
                   jax.ShapeDtypeStruct((n, LANES), F32)],
        compiler_params=_params(("arbitrary",)),
        name="outproj",
    )(hc, r, x2, w_out_bf, gate1, ln1_g, ln1_b, scale2, shift2, w_router, b_router)


def _plan_kernel(route_ref, pos_ref, meta_ref, tri, upper, run, pstart):
    ph = pl.program_id(0)
    i = pl.program_id(1)
    tb = PLAN_TB
    lane = lax.broadcasted_iota(jnp.int32, (tb, LANES), 1).astype(F32)
    e1 = route_ref[:, 0:1]
    e2 = route_ref[:, 1:2]
    member = (lane == e1) | (lane == e2)
    col_count = jnp.sum(member.astype(F32), axis=0, keepdims=True)

    @pl.when((ph == 0) & (i == 0))
    def _():
        r = lax.broadcasted_iota(jnp.int32, (tb, tb), 0)
        c = lax.broadcasted_iota(jnp.int32, (tb, tb), 1)
        tri[...] = (c < r).astype(BF16)
        ru = lax.broadcasted_iota(jnp.int32, (LANES, LANES), 0)
        cu = lax.broadcasted_iota(jnp.int32, (LANES, LANES), 1)
        upper[...] = (ru < cu).astype(BF16)
        run[...] = jnp.zeros(run.shape, F32)

    @pl.when(ph == 0)
    def _():
        run[...] += col_count

    @pl.when((ph == 0) & (i == pl.num_programs(1) - 1))
    def _():
        counts = run[...]
        blocks = jnp.floor((counts + (MOE_TM - 1)) * (1.0 / MOE_TM))
        start_blocks = jnp.dot(jnp.broadcast_to(blocks, (SUBLANES, LANES)).astype(BF16), upper[...],
                               preferred_element_type=F32)[0:1, :]
        pstart[...] = start_blocks * MOE_TM
        row = lax.broadcasted_iota(jnp.int32, (SUBLANES, LANES), 0)
        meta_ref[...] = jnp.where(row == 0, counts, jnp.where(row == 1, blocks,
                                  jnp.where(row == 2, start_blocks, 0.0)))
        run[...] = jnp.zeros(run.shape, F32)

    @pl.when(ph == 1)
    def _():
        earlier = jnp.dot(tri[...], member.astype(BF16), preferred_element_type=F32) + run[...]
        dest = earlier + pstart[...]
        p1 = jnp.sum(jnp.where(lane == e1, dest, 0.0), axis=-1, keepdims=True)
        p2 = jnp.sum(jnp.where(lane == e2, dest, 0.0), axis=-1, keepdims=True)
        pos_ref[...] = jnp.where(lane == 0, p1, jnp.where(lane == 1, p2, 0.0)).astype(jnp.int32)
        run[...] += col_count


def _plan(route):
    n = route.shape[0]
    tb = PLAN_TB
    return pl.pallas_call(
        _plan_kernel,
        grid=(2, n // tb),
        in_specs=[pl.BlockSpec((tb, LANES), lambda ph, i: (i, 0))],
        out_specs=[pl.BlockSpec((tb, LANES), lambda ph, i: (ph * i, 0)),
                   pl.BlockSpec((SUBLANES, LANES), lambda ph, i: (0, 0))],
        out_shape=[jax.ShapeDtypeStruct((n, LANES), jnp.int32),
                   jax.ShapeDtypeStruct((SUBLANES, LANES), F32)],
        scratch_shapes=[pltpu.VMEM((tb, tb), BF16),
                        pltpu.VMEM((LANES, LANES), BF16),
                        pltpu.VMEM((1, LANES), F32),
                        pltpu.VMEM((1, LANES), F32)],
        compiler_params=_params(("arbitrary", "arbitrary")),
        name="plan",
    )(route)


MOE_COMPUTE, MOE_CAST, MOE_ZERO = 1, 2, 4
MOE_PAR_COMPUTE_SHIFT, MOE_PAR_CAST_SHIFT = 3, 4


def _moe_kernel(flag_ref, rb_ref, fe_ref, fc_ref, nu_ref, idx_ref, idx_next_ref, u2_hbm,
                wg_ref, wu_ref, wd_ref, y_ref, wgb, wub, wdb, xbuf, sem):
    s = pl.program_id(0)
    flags = flag_ref[s]
    n_used = nu_ref[0]
    rb = rb_ref[s]
    slot = rb % 2

    def start_gather(rows_ref, dst):
        for r in range(MOE_TM):
            pltpu.make_async_copy(u2_hbm.at[pl.ds(rows_ref[0, 0, r], TOK_ROWS)],
                                  xbuf.at[dst, pl.ds(r * TOK_PITCH, TOK_ROWS)], sem.at[dst]).start()

    @pl.when(s == 0)
    def _():
        start_gather(idx_ref, 0)

    @pl.when((flags & MOE_CAST) != 0)
    def _():
        par = (flags >> MOE_PAR_CAST_SHIFT) & 1
        c = fc_ref[s]
        wgb[par, c] = wg_ref[0].astype(BF16)
        wub[par, c] = wu_ref[0].astype(BF16)
        wdb[par, c] = wd_ref[0].astype(BF16)

    @pl.when((flags & MOE_COMPUTE) != 0)
    def _():
        n_rows = MOE_TM * TOK_ROWS
        pltpu.make_async_copy(u2_hbm.at[pl.ds(0, n_rows)], xbuf.at[slot, pl.ds(0, n_rows)],
                              sem.at[slot]).wait()

        @pl.when(rb + 1 < n_used)
        def _():
            start_gather(idx_next_ref, 1 - slot)

        par = (flags >> MOE_PAR_COMPUTE_SHIFT) & 1
        x = _load_token_major(xbuf.at[slot], MOE_TM).astype(BF16)
        acc = jnp.zeros((MOE_TM, D_MODEL), F32)
        for c in range(MOE_CHUNKS):
            hg = jnp.dot(x, wgb[par, c], preferred_element_type=F32)
            hu = jnp.dot(x, wub[par, c], preferred_element_type=F32)
            hid = (_silu(hg) * hu).astype(BF16)
            acc = acc + jnp.dot(hid, wdb[par, c], preferred_element_type=F32)
        _store_token_major(y_ref, acc)

    @pl.when((flags & MOE_ZERO) != 0)
    def _():
        y_ref[...] = jnp.zeros(y_ref.shape, F32)


def _moe(steps, slot_tok3, u2, w_gate, w_up, w_down):
    flags, rb, fetch_e, fetch_c, n_used = steps
    nb = slot_tok3.shape[0]
    d = w_gate.shape[1]
    assert u2.shape[1] == LANES and d == D_MODEL
    tm, fc = MOE_TM, MOE_FC
    smem_idx = lambda fn: pl.BlockSpec((1, 1, tm), fn, memory_space=pltpu.SMEM)
    grid_spec = pltpu.PrefetchScalarGridSpec(
        num_scalar_prefetch=5,
        grid=(flags.shape[0],),
        in_specs=[smem_idx(lambda s, fl, rb, fe, fc_, nu: (rb[s], 0, 0)),
                  smem_idx(lambda s, fl, rb, fe, fc_, nu: (jnp.minimum(rb[s] + 1, nb - 1), 0, 0)),
                  pl.BlockSpec(memory_space=pl.ANY),
                  pl.BlockSpec((1, d, fc), lambda s, fl, rb, fe, fc_, nu: (fe[s], 0, fc_[s])),
                  pl.BlockSpec((1, d, fc), lambda s, fl, rb, fe, fc_, nu: (fe[s], 0, fc_[s])),
                  pl.BlockSpec((1, fc, d), lambda s, fl, rb, fe, fc_, nu: (fe[s], fc_[s], 0))],
        out_specs=pl.BlockSpec((tm * TOK_ROWS, LANES), lambda s, fl, rb, fe, fc_, nu: (rb[s], 0)),
        scratch_shapes=[pltpu.VMEM((2, MOE_CHUNKS, d, fc), BF16),
                        pltpu.VMEM((2, MOE_CHUNKS, d, fc), BF16),
                        pltpu.VMEM((2, MOE_CHUNKS, fc, d), BF16),
                        pltpu.VMEM((2, tm * TOK_PITCH, LANES), F32),
                        pltpu.SemaphoreType.DMA((2,))],
    )
    return pl.pallas_call(
        _moe_kernel,
        grid_spec=grid_spec,
        out_shape=jax.ShapeDtypeStruct((nb * tm * TOK_ROWS, LANES), F32),
        compiler_params=_params(("arbitrary",)),
        name="moe",
    )(flags, rb, fetch_e, fetch_c, n_used, slot_tok3, slot_tok3, u2, w_gate, w_up, w_down)


def _combine_kernel(pos_ref, pos_next_ref, y_hbm, x1_ref, route_ref, gate_ref, g_ref, b_ref,
                    o_ref, ybuf, sem):
    i = pl.program_id(0)
    n_steps = pl.num_programs(0)
    slot = i % 2

    def start_gather(rows_ref, s):
        for r in range(COMB_TM):
            for kk in range(TOP_K):
                pltpu.make_async_copy(y_hbm.at[pl.ds(rows_ref[0, 0, TOP_K * r + kk], TOK_ROWS)],
                                      ybuf.at[s, kk, pl.ds(r * TOK_PITCH, TOK_ROWS)],
                                      sem.at[s]).start()

    @pl.when(i == 0)
    def _():
        start_gather(pos_ref, 0)

    n_rows = COMB_TM * TOK_ROWS
    for kk in range(TOP_K):
        pltpu.make_async_copy(y_hbm.at[pl.ds(0, n_rows)], ybuf.at[slot, kk, pl.ds(0, n_rows)],
                              sem.at[slot]).wait()

    @pl.when(i + 1 < n_steps)
    def _():
        start_gather(pos_next_ref, 1 - slot)

    w1 = route_ref[:, 2:3]
    w2 = route_ref[:, 3:4]
    ffn = (w1 * _load_token_major(ybuf.at[slot, 0], COMB_TM)
           + w2 * _load_token_major(ybuf.at[slot, 1], COMB_TM))
    y = DN_ALPHA * x1_ref[...] + (1.0 + gate_ref[...]) * ffn
    o_ref[...] = _ln(y) * g_ref[...] + b_ref[...]


def _combine(pos3, y, x1, route, gate2, ln2_g, ln2_b, seq):
    n, d = x1.shape
    tm = COMB_TM
    steps = n // tm
    tiles_per_seq = seq // tm
    row = lambda i: (i, 0)
    const = lambda i: (0, 0)
    smem_idx = lambda fn: pl.BlockSpec((1, 1, TOP_K * tm), fn, memory_space=pltpu.SMEM)
    return pl.pallas_call(
        _combine_kernel,
        grid=(steps,),
        in_specs=[smem_idx(lambda i: (i, 0, 0)),
                  smem_idx(lambda i: (jnp.minimum(i + 1, steps - 1), 0, 0)),
                  pl.BlockSpec(memory_space=pl.ANY),
                  pl.BlockSpec((tm, d), row),
                  pl.BlockSpec((tm, LANES), row),
                  pl.BlockSpec((None, 1, d), lambda i: (i // tiles_per_seq, 0, 0)),
                  pl.BlockSpec((1, d), const),
                  pl.BlockSpec((1, d), const)],
        out_specs=pl.BlockSpec((tm, d), row),
        out_shape=jax.ShapeDtypeStruct((n, d), F32),
        scratch_shapes=[pltpu.VMEM((2, TOP_K, tm * TOK_PITCH, LANES), F32),
                        pltpu.SemaphoreType.DMA((2,))],
        compiler_params=_params(("arbitrary",)),
        name="combine",
    )(pos3, pos3, y, x1, route, gate2, ln2_g, ln2_b)


def _retention_tables():
    h = jnp.arange(RET_HEADS, dtype=F32)
    log_gamma = jnp.log1p(-jnp.exp2(-5.0 - h))
    idx = jnp.arange(RET_CHUNK, dtype=F32)
    diff = idx[:, None] - idx[None, :]
    causal = diff >= 0
    dec = jnp.where(causal[None],
                    jnp.exp(jnp.where(causal, diff, 0.0)[None] * log_gamma[:, None, None]), 0.0)
    xi = jnp.exp((idx[None, :] + 1.0) * log_gamma[:, None])
    zeta = jnp.exp((RET_CHUNK - 1.0 - idx[None, :]) * log_gamma[:, None])
    cd = jnp.exp(RET_CHUNK * log_gamma)
    bc = lambda a: jnp.broadcast_to(a[:, :, None], (RET_HEADS, RET_CHUNK, HEAD_DIM))
    return dec, bc(xi), bc(zeta), jnp.broadcast_to(cd[:, None, None], (RET_HEADS, 1, HEAD_DIM))


def _rope_tables():
    half = HEAD_DIM // 2
    inv_freq = jnp.exp(-math.log(ROPE_BASE) * jnp.arange(half, dtype=F32) / half)
    invf = jnp.concatenate([inv_freq, inv_freq])[None, :]
    sgn = jnp.concatenate([-jnp.ones((half,), F32), jnp.ones((half,), F32)])[None, :]
    return invf, sgn


def _step_tables(meta, nb):
    i32 = jnp.int32
    blocks = meta[1, :N_EXPERTS].astype(i32)
    first_block = meta[2, :N_EXPERTS].astype(i32)
    e_iota = jnp.arange(N_EXPERTS, dtype=i32)
    present = blocks > 0
    n_used = jnp.sum(blocks)
    seg_len = jnp.where(present, jnp.maximum(blocks, MOE_CHUNKS), 0)
    seg_end = MOE_CHUNKS + jnp.cumsum(seg_len)
    seg_start = seg_end - seg_len
    s_used = seg_end[-1]
    ordinal = jnp.cumsum(present.astype(i32)) - present.astype(i32)
    later = present[None, :] & (e_iota[None, :] > e_iota[:, None])
    next_e = jnp.min(jnp.where(later, e_iota[None, :], N_EXPERTS), axis=1)
    first_e = jnp.min(jnp.where(present, e_iota, N_EXPERTS))
    last_e = jnp.max(jnp.where(present, e_iota, -1))

    n_steps = MOE_CHUNKS + nb + (MOE_CHUNKS - 1) * (N_EXPERTS - 1)
    s = jnp.arange(n_steps, dtype=i32)
    prologue = s < MOE_CHUNKS
    in_seg = (~prologue) & (s < s_used)
    e_s = jnp.minimum(jnp.sum((seg_end[None, :] <= s[:, None]).astype(i32), axis=1), N_EXPERTS - 1)
    onehot = (e_s[:, None] == e_iota[None, :]).astype(i32)
    pick = lambda t: jnp.sum(onehot * t[None, :], axis=1)
    k = s - pick(seg_start)
    nblk = pick(blocks)
    has_next = pick(next_e) < N_EXPERTS
    compute = in_seg & (k < nblk)
    rb_seg = pick(first_block) + jnp.minimum(k, nblk)
    rb_tail = n_used + (s - s_used)
    rb = jnp.where(prologue, 0, jnp.where(in_seg, rb_seg, rb_tail))
    zero = (s >= s_used) & (rb_tail < nb)
    rb = jnp.clip(rb, 0, nb - 1)
    cast = prologue | (in_seg & has_next & (k < MOE_CHUNKS))
    fetch_e = jnp.where(prologue, first_e,
                        jnp.where(in_seg, jnp.where(has_next, pick(next_e), e_s), last_e))
    fetch_c = jnp.where(prologue, s, jnp.where(in_seg & has_next,
                                               jnp.minimum(k, MOE_CHUNKS - 1), MOE_CHUNKS - 1))
    par_compute = pick(ordinal) % 2
    par_cast = jnp.where(prologue, 0, (pick(ordinal) + 1) % 2)
    flags = (compute.astype(i32) * MOE_COMPUTE + cast.astype(i32) * MOE_CAST
             + zero.astype(i32) * MOE_ZERO + (par_compute << MOE_PAR_COMPUTE_SHIFT)
             + (par_cast << MOE_PAR_CAST_SHIFT))
    return flags, rb, fetch_e, fetch_c, n_used.reshape(1)


def _row_tokens(meta, expert_ids, n_tok, nb):
    counts = meta[0, :N_EXPERTS].astype(jnp.int32)
    blocks = meta[1, :N_EXPERTS].astype(jnp.int32)
    n_slots = n_tok * TOP_K
    fill = blocks * MOE_TM - counts
    e_iota = jnp.arange(N_EXPERTS, dtype=jnp.int32)[:, None]
    r_iota = jnp.arange(MOE_TM, dtype=jnp.int32)[None, :]
    filler_keys = jnp.where(r_iota < fill[:, None], e_iota, N_EXPERTS).reshape(-1)
    order = jnp.argsort(jnp.concatenate([expert_ids.reshape(-1), filler_keys])).astype(jnp.int32)
    slot_tok = jnp.where(order < n_slots, order // TOP_K, 0)
    return (slot_tok * TOK_ROWS).reshape(nb, 1, MOE_TM)


def kernel(x, c, positions, w_ada, b_ada, w_in, conv_w, conv_b, conv_ln_g, conv_ln_b, w_out,
           ln1_g, ln1_b, w_group_router, b_group_router, w_expert_router, b_expert_router,
           w_gate, w_up, w_down, ln2_g, ln2_b):
    batch, seq, d = x.shape
    n_tok = batch * seq
    l = 0
    row = lambda a: a[l][None, :]

    c_pad = jnp.zeros((8, d), F32).at[:batch].set(c)
    mod = _ada(c_pad, w_ada[l], b_ada[l][None, :])[:batch]
    shift1, scale1, gate1, shift2, scale2, gate2 = [m[:, None, :] for m in jnp.split(mod, 6, axis=-1)]

    x2 = x.reshape(n_tok, d)
    invf, sgn = _rope_tables()
    h0, q, k, v, gs = _inproj(x2, scale1, shift1, positions.reshape(n_tok, 1), invf, sgn,
                              w_in[l].astype(BF16), seq)

    hc = _conv(h0.reshape(batch, seq, D_CONV), conv_w[l], row(conv_b), row(conv_ln_g),
               row(conv_ln_b)).reshape(n_tok, D_CONV)

    dec, xi, zeta, cd = _retention_tables()
    r = _retention(q, k, v, gs, dec, xi, zeta, cd, batch, seq)

    n_route = N_GROUPS + N_EXPERTS
    w_router = jnp.zeros((d, LANES), F32).at[:, :N_GROUPS].set(w_group_router[l])
    w_router = w_router.at[:, N_GROUPS:n_route].set(w_expert_router[l])
    b_router = jnp.zeros((1, LANES), F32).at[0, :N_GROUPS].set(b_group_router[l])
    b_router = b_router.at[0, N_GROUPS:n_route].set(b_expert_router[l])
    x1, u2, route = _outproj(hc, r, x2, w_out[l].astype(BF16), gate1, row(ln1_g), row(ln1_b),
                             scale2, shift2, w_router, b_router, seq)

    n_slots = n_tok * TOP_K
    nb = n_slots // MOE_TM + N_EXPERTS
    assert N_EXPERTS * (MOE_CHUNKS - 1) * MOE_TM < n_slots
    pos_pad, meta = _plan(route)
    slot_tok3 = _row_tokens(meta, route[:, :TOP_K].astype(jnp.int32), n_tok, nb)
    y = _moe(_step_tables(meta, nb), slot_tok3, u2, w_gate[l], w_up[l], w_down[l])

    pos3 = (pos_pad[:, :TOP_K] * TOK_ROWS).reshape(n_tok // COMB_TM, 1, TOP_K * COMB_TM)
    out = _combine(pos3, y, x1, route, gate2, row(ln2_g), row(ln2_b), seq)
    return out.reshape(batch, seq, d)
```

```python
import functools
import math

import jax
import jax.numpy as jnp
from jax import lax
from jax.experimental import pallas as pl
from jax.experimental.pallas import tpu as pltpu

F32 = jnp.float32
BF16 = jnp.bfloat16

D_MODEL = 2048
D_CONV = 1024
D_RET = 1024
CONV_WIDTH = 31
RET_HEADS = 8
HEAD_DIM = 128
RET_CHUNK = 128
ROPE_BASE = 10000.0
N_GROUPS = 4
EXPERTS_PER_GROUP = 8
N_EXPERTS = 32
TOP_K = 2
D_EXPERT = 1024
LN_EPS = 1e-5
DEPTH = 1
DN_ALPHA = (2 * DEPTH) ** 0.25

LANES = 128
SUBLANES = 8
TOK_ROWS = D_MODEL // LANES
TOK_PITCH = TOK_ROWS + SUBLANES
VMEM_LIMIT = 56 * 1024 * 1024
MOE_VMEM_LIMIT = 60 * 1024 * 1024

ADA_TN = 1024
INPROJ_TM = 512
INPROJ_TN = 1024
CONV_TT = 512
CONV_HALO = 32
CONV_ROWS = 128
CONV_LANES = 128
RET_TT = 256
OUTPROJ_TM = 256
PLAN_TB = 1024
MOE_TM = 256
MOE_FC = 256
MOE_CHUNKS = D_EXPERT // MOE_FC
MOE_XBUFS = 3
COMB_TM = 256


def _ln(x):
    mu = jnp.mean(x, axis=-1, keepdims=True)
    xc = x - mu
    var = jnp.mean(xc * xc, axis=-1, keepdims=True)
    return xc * lax.rsqrt(var + LN_EPS)


def _silu(x):
    return x * jax.nn.sigmoid(x)


def _store_token_major(ref, val):
    rows = val.shape[0]
    for s in range(TOK_ROWS):
        ref[pl.ds(s, rows, stride=TOK_ROWS), :] = val[:, s * LANES:(s + 1) * LANES]


def _load_token_major(ref, rows):
    return jnp.concatenate([ref[pl.ds(s, rows, stride=TOK_PITCH), :] for s in range(TOK_ROWS)],
                           axis=1)


def _params(sem):
    return pltpu.CompilerParams(dimension_semantics=sem, vmem_limit_bytes=VMEM_LIMIT)


def _ada_kernel(c_ref, w_ref, b_ref, o_ref):
    ca = _silu(c_ref[...]).astype(BF16)
    o_ref[...] = jnp.dot(ca, w_ref[...].astype(BF16), preferred_element_type=F32) + b_ref[...]


def _ada(c_pad, w_ada, b_ada):
    rows, d = c_pad.shape
    n = w_ada.shape[1]
    return pl.pallas_call(
        _ada_kernel,
        grid=(n // ADA_TN,),
        in_specs=[pl.BlockSpec((rows, d), lambda j: (0, 0)),
                  pl.BlockSpec((d, ADA_TN), lambda j: (0, j)),
                  pl.BlockSpec((1, ADA_TN), lambda j: (0, j))],
        out_specs=pl.BlockSpec((rows, ADA_TN), lambda j: (0, j)),
        out_shape=jax.ShapeDtypeStruct((rows, n), F32),
        compiler_params=_params(("arbitrary",)),
        name="ada",
    )(c_pad, w_ada, b_ada)


def _inproj_kernel(x_ref, sc_ref, sh_ref, pos_ref, invf_ref, sgn_ref, w_ref,
                   h0_ref, q_ref, k_ref, v_ref, g_ref,
                   u_scr, cos_scr, sin_scr, acc0, acc1, acc2):
    i = pl.program_id(0)
    j = pl.program_id(1)
    cur = i % 2
    prev = 1 - cur

    def normalise():
        u = _ln(x_ref[...]) * (1.0 + sc_ref[...]) + sh_ref[...]
        u_scr[cur] = u.astype(BF16)
        ang = pos_ref[...].astype(F32) * invf_ref[...]
        cos_scr[cur] = jnp.cos(ang)
        sin_scr[cur] = jnp.sin(ang) * sgn_ref[...]

    def matmul():
        return jnp.dot(u_scr[prev], w_ref[...], preferred_element_type=F32)

    def rotary_to(out_ref, src, scale):
        cos = cos_scr[prev]
        sin = sin_scr[prev]
        for h in range(RET_HEADS):
            xs = src[:, h * HEAD_DIM:(h + 1) * HEAD_DIM]
            rot = xs * cos + pltpu.roll(xs, HEAD_DIM // 2, axis=1) * sin
            if scale is not None:
                rot = rot * scale
            out_ref[:, h * HEAD_DIM:(h + 1) * HEAD_DIM] = rot.astype(out_ref.dtype)

    @pl.when((j == 0) & (i == 0))
    def _():
        normalise()

    @pl.when((j == 0) & (i > 0))
    def _():
        acc0[...] = matmul()
        normalise()

    @pl.when((j == 1) & (i > 0))
    def _():
        acc1[...] = matmul()

    @pl.when((j == 2) & (i > 0))
    def _():
        acc2[...] = matmul()
        h0_ref[...] = acc0[...] * jax.nn.sigmoid(acc1[...])

    @pl.when((j == 3) & (i > 0))
    def _():
        acc0[...] = matmul()
        rotary_to(q_ref, acc2, None)

    @pl.when((j == 4) & (i > 0))
    def _():
        acc1[...] = matmul()
        rotary_to(k_ref, acc0, HEAD_DIM ** -0.5)

    @pl.when((j == 5) & (i > 0))
    def _():
        res = matmul()
        v_ref[...] = acc1[...].astype(BF16)
        g_ref[...] = _silu(res)


def _inproj(x2, scale1, shift1, pos2, invf, sgn, w_in_bf, seq):
    n, d = x2.shape
    tm, tn = INPROJ_TM, INPROJ_TN
    tiles_per_seq = seq // tm
    n_tiles = n // tm
    assert w_in_bf.shape[1] == 6 * tn
    norm_tile = lambda i: jnp.minimum(i, n_tiles - 1)
    row_in = lambda i, j: (norm_tile(i), 0)
    per_batch = lambda i, j: (norm_tile(i) // tiles_per_seq, 0, 0)
    row_out = lambda i, j: (jnp.maximum(i - 1, 0), 0)
    const = lambda i, j: (0, 0)
    out_shape = [jax.ShapeDtypeStruct((n, D_CONV), F32),
                 jax.ShapeDtypeStruct((n, D_RET), BF16),
                 jax.ShapeDtypeStruct((n, D_RET), BF16),
                 jax.ShapeDtypeStruct((n, D_RET), BF16),
                 jax.ShapeDtypeStruct((n, D_RET), F32)]
    return pl.pallas_call(
        _inproj_kernel,
        grid=(n_tiles + 1, 6),
        in_specs=[pl.BlockSpec((tm, d), row_in),
                  pl.BlockSpec((None, 1, d), per_batch),
                  pl.BlockSpec((None, 1, d), per_batch),
                  pl.BlockSpec((tm, 1), row_in),
                  pl.BlockSpec((1, LANES), const),
                  pl.BlockSpec((1, LANES), const),
                  pl.BlockSpec((d, tn), lambda i, j: (0, jnp.where(i > 0, j, 0)))],
        out_specs=[pl.BlockSpec((tm, tn), row_out) for _ in out_shape],
        out_shape=out_shape,
        scratch_shapes=[pltpu.VMEM((2, tm, d), BF16),
                        pltpu.VMEM((2, tm, LANES), F32),
                        pltpu.VMEM((2, tm, LANES), F32),
                        pltpu.VMEM((tm, tn), F32),
                        pltpu.VMEM((tm, tn), F32),
                        pltpu.VMEM((tm, tn), F32)],
        compiler_params=_params(("arbitrary", "arbitrary")),
        name="inproj",
    )(x2, scale1, shift1, pos2, invf, sgn, w_in_bf)


def _conv_kernel(main_ref, halo_ref, w_ref, b_ref, g_ref, beta_ref, o_ref, buf, cbuf):
    t = pl.program_id(1)
    buf[0:CONV_HALO, :] = jnp.where(t > 0, halo_ref[...], 0.0)
    buf[CONV_HALO:, :] = main_ref[...]
    first = CONV_HALO - (CONV_WIDTH - 1)
    rows = CONV_ROWS

    def chunk(ci, carry):
        r0 = pl.multiple_of(ci * rows, rows)
        for lg in range(D_CONV // CONV_LANES):
            lanes = slice(lg * CONV_LANES, (lg + 1) * CONV_LANES)
            acc = None
            for s in range(SUBLANES):
                taps = [j for j in range(CONV_WIDTH) if (first + j) % SUBLANES == s]
                span = rows if s == 0 else rows + SUBLANES
                part = None
                for j in taps:
                    a0 = first + j - s
                    term = w_ref[j:j + 1, lanes] * buf[pl.ds(r0 + a0, span), lanes]
                    part = term if part is None else part + term
                part = part[s:s + rows, :]
                acc = part if acc is None else acc + part
            cbuf[pl.ds(r0, rows), lanes] = acc
        return carry

    lax.fori_loop(0, CONV_TT // rows, chunk, 0)
    y = _ln(cbuf[...] + b_ref[...]) * g_ref[...] + beta_ref[...]
    o_ref[...] = _silu(y).astype(BF16)


def _conv(h0, conv_w, conv_b, ln_g, ln_b):
    b, t, c = h0.shape
    tt = CONV_TT
    ratio = tt // CONV_HALO
    const = lambda bi, ti: (0, 0)
    return pl.pallas_call(
        _conv_kernel,
        grid=(b, t // tt),
        in_specs=[pl.BlockSpec((None, tt, c), lambda bi, ti: (bi, ti, 0)),
                  pl.BlockSpec((None, CONV_HALO, c),
                               lambda bi, ti: (bi, jnp.maximum(ti * ratio - 1, 0), 0)),
                  pl.BlockSpec((CONV_WIDTH, c), const),
                  pl.BlockSpec((1, c), const),
                  pl.BlockSpec((1, c), const),
                  pl.BlockSpec((1, c), const)],
        out_specs=pl.BlockSpec((None, tt, c), lambda bi, ti: (bi, ti, 0)),
        out_shape=jax.ShapeDtypeStruct((b, t, c), BF16),
        scratch_shapes=[pltpu.VMEM((tt + CONV_HALO, c), F32),
                        pltpu.VMEM((tt, c), F32)],
        compiler_params=_params(("arbitrary", "arbitrary")),
        name="conv",
    )(h0, h0, conv_w, conv_b, ln_g, ln_b)


def _ret_kernel(q_ref, k_ref, v_ref, gs_ref, dec_ref, xi_ref, zeta_ref, cd_ref, o_ref, st):
    t = pl.program_id(1)

    @pl.when(t == 0)
    def _():
        st[...] = jnp.zeros(st.shape, F32)

    for c in range(RET_TT // RET_CHUNK):
        rows = slice(c * RET_CHUNK, (c + 1) * RET_CHUNK)
        for h in range(RET_HEADS):
            cols = slice(h * HEAD_DIM, (h + 1) * HEAD_DIM)
            q = q_ref[rows, cols]
            k = k_ref[rows, cols]
            v = v_ref[rows, cols]
            s = lax.dot_general(q, k, (((1,), (1,)), ((), ())), preferred_element_type=F32)
            s = s * dec_ref[h]
            inner = jnp.dot(s.astype(BF16), v, preferred_element_type=F32)
            state = st[h]
            qx = (q.astype(F32) * xi_ref[h]).astype(BF16)
            cross = jnp.dot(qx, state.astype(BF16), preferred_element_type=F32)
            kz = (k.astype(F32) * zeta_ref[h]).astype(BF16)
            kv = lax.dot_general(kz, v, (((0,), (0,)), ((), ())), preferred_element_type=F32)
            st[h] = state * cd_ref[h] + kv
            r = _ln(inner + cross)
            o_ref[rows, cols] = (gs_ref[rows, cols] * r).astype(BF16)


def _retention(q, k, v, gs, dec, xi, zeta, cd, batch, seq):
    n, w = q.shape
    tt = RET_TT
    tiles = seq // tt
    row = lambda bi, ti: (bi * tiles + ti, 0)
    const3 = lambda bi, ti: (0, 0, 0)
    tab = pl.BlockSpec((RET_HEADS, RET_CHUNK, HEAD_DIM), const3)
    return pl.pallas_call(
        _ret_kernel,
        grid=(batch, tiles),
        in_specs=[pl.BlockSpec((tt, w), row)] * 4 + [
            tab, tab, tab, pl.BlockSpec((RET_HEADS, 1, HEAD_DIM), const3)],
        out_specs=pl.BlockSpec((tt, w), row),
        out_shape=jax.ShapeDtypeStruct((n, w), BF16),
        scratch_shapes=[pltpu.VMEM((RET_HEADS, HEAD_DIM, HEAD_DIM), F32)],
        compiler_params=_params(("arbitrary", "arbitrary")),
        name="retention",
    )(q, k, v, gs, dec, xi, zeta, cd)


def _outproj_kernel(hc_ref, r_ref, x_ref, w_ref, gate_ref, g1_ref, b1_ref, sc2_ref, sh2_ref,
                    wr_ref, br_ref, x1_ref, u2_ref, route_ref):
    mix = jnp.dot(hc_ref[...], w_ref[0:D_CONV, :], preferred_element_type=F32)
    mix = mix + jnp.dot(r_ref[...], w_ref[D_CONV:, :], preferred_element_type=F32)
    y = DN_ALPHA * x_ref[...] + (1.0 + gate_ref[...]) * mix
    x1 = _ln(y) * g1_ref[...] + b1_ref[...]
    x1_ref[...] = x1
    u2 = _ln(x1) * (1.0 + sc2_ref[...]) + sh2_ref[...]
    _store_token_major(u2_ref, u2)
    logits = jnp.dot(u2.astype(BF16), wr_ref[...].astype(BF16),
                     preferred_element_type=F32) + br_ref[...]

    lane = lax.broadcasted_iota(jnp.int32, logits.shape, 1).astype(F32)
    neg = jnp.float32(-jnp.inf)
    big = jnp.float32(LANES)
    first_where = lambda m: jnp.min(jnp.where(m, lane, big), axis=-1, keepdims=True)

    is_g = lane < N_GROUPS
    gl = jnp.where(is_g, logits, neg)
    gmax = jnp.max(gl, axis=-1, keepdims=True)
    grp = first_where(gl == gmax)
    gsum = jnp.sum(jnp.where(is_g, jnp.exp(gl - gmax), 0.0), axis=-1, keepdims=True)
    g_w = 1.0 / gsum

    lo = N_GROUPS + grp * EXPERTS_PER_GROUP
    el = jnp.where((lane >= lo) & (lane < lo + EXPERTS_PER_GROUP), logits, neg)
    m1 = jnp.max(el, axis=-1, keepdims=True)
    i1 = first_where(el == m1)
    el2 = jnp.where(lane == i1, neg, el)
    m2 = jnp.max(el2, axis=-1, keepdims=True)
    i2 = first_where(el2 == m2)
    e21 = jnp.exp(m2 - m1)
    w1 = g_w / (1.0 + e21)
    w2 = g_w * e21 / (1.0 + e21)
    route = jnp.where(lane == 0, i1 - N_GROUPS,
                      jnp.where(lane == 1, i2 - N_GROUPS,
                                jnp.where(lane == 2, w1, jnp.where(lane == 3, w2, 0.0))))
    route_ref[...] = route


def _outproj(hc, r, x2, w_out_bf, gate1, ln1_g, ln1_b, scale2, shift2, w_router, b_router, seq):
    n, d = x2.shape
    tm = OUTPROJ_TM
    tiles_per_seq = seq // tm
    row = lambda i: (i, 0)
    per_batch = lambda i: (i // tiles_per_seq, 0, 0)
    const = lambda i: (0, 0)
    vec = pl.BlockSpec((1, d), const)
    mod = pl.BlockSpec((None, 1, d), per_batch)
    return pl.pallas_call(
        _outproj_kernel,
        grid=(n // tm,),
        in_specs=[pl.BlockSpec((tm, D_CONV), row),
                  pl.BlockSpec((tm, D_RET), row),
                  pl.BlockSpec((tm, d), row),
                  pl.BlockSpec((d, d), const),
                  mod, vec, vec, mod, mod,
                  pl.BlockSpec((d, LANES), const),
                  pl.BlockSpec((1, LANES), const)],
        out_specs=[pl.BlockSpec((tm, d), row),
                   pl.BlockSpec((tm * TOK_ROWS, LANES), row),
                   pl.BlockSpec((tm, LANES), row)],
        out_shape=[jax.ShapeDtypeStruct((n, d), F32),
                   jax.ShapeDtypeStruct((n * TOK_ROWS, LANES), F32),
                   jax.ShapeDtypeStruct((n, LANES), F32)],
        compiler_params=_params(("arbitrary",)),
        name="outproj",
    )(hc, r, x2, w_out_bf, gate1, ln1_g, ln1_b, scale2, shift2, w_router, b_router)


def _plan_kernel(route_ref, pos_ref, meta_ref, tri, upper, run, pstart):
    ph = pl.program_id(0)
    i = pl.program_id(1)
    tb = PLAN_TB
    lane = lax.broadcasted_iota(jnp.int32, (tb, LANES), 1).astype(F32)
    e1 = route_ref[:, 0:1]
    e2 = route_ref[:, 1:2]
    member = (lane == e1) | (lane == e2)
    col_count = jnp.sum(member.astype(F32), axis=0, keepdims=True)

    @pl.when((ph == 0) & (i == 0))
    def _():
        r = lax.broadcasted_iota(jnp.int32, (tb, tb), 0)
        c = lax.broadcasted_iota(jnp.int32, (tb, tb), 1)
        tri[...] = (c < r).astype(BF16)
        ru = lax.broadcasted_iota(jnp.int32, (LANES, LANES), 0)
        cu = lax.broadcasted_iota(jnp.int32, (LANES, LANES), 1)
        upper[...] = (ru < cu).astype(BF16)
        run[...] = jnp.zeros(run.shape, F32)

    @pl.when(ph == 0)
    def _():
        run[...] += col_count

    @pl.when((ph == 0) & (i == pl.num_programs(1) - 1))
    def _():
        counts = run[...]
        blocks = jnp.floor((counts + (MOE_TM - 1)) * (1.0 / MOE_TM))
        start_blocks = jnp.dot(jnp.broadcast_to(blocks, (SUBLANES, LANES)).astype(BF16), upper[...],
                               preferred_element_type=F32)[0:1, :]
        pstart[...] = start_blocks * MOE_TM
        row = lax.broadcasted_iota(jnp.int32, (SUBLANES, LANES), 0)
        meta_ref[...] = jnp.where(row == 0, counts, jnp.where(row == 1, blocks,
                                  jnp.where(row == 2, start_blocks, 0.0)))
        run[...] = jnp.zeros(run.shape, F32)

    @pl.when(ph == 1)
    def _():
        earlier = jnp.dot(tri[...], member.astype(BF16), preferred_element_type=F32) + run[...]
        dest = earlier + pstart[...]
        p1 = jnp.sum(jnp.where(lane == e1, dest, 0.0), axis=-1, keepdims=True)
        p2 = jnp.sum(jnp.where(lane == e2, dest, 0.0), axis=-1, keepdims=True)
        pos_ref[...] = jnp.where(lane == 0, p1, jnp.where(lane == 1, p2, 0.0)).astype(jnp.int32)
        run[...] += col_count


def _plan(route):
    n = route.shape[0]
    tb = PLAN_TB
    return pl.pallas_call(
        _plan_kernel,
        grid=(2, n // tb),
        in_specs=[pl.BlockSpec((tb, LANES), lambda ph, i: (i, 0))],
        out_specs=[pl.BlockSpec((tb, LANES), lambda ph, i: (ph * i, 0)),
                   pl.BlockSpec((SUBLANES, LANES), lambda ph, i: (0, 0))],
        out_shape=[jax.ShapeDtypeStruct((n, LANES), jnp.int32),
                   jax.ShapeDtypeStruct((SUBLANES, LANES), F32)],
        scratch_shapes=[pltpu.VMEM((tb, tb), BF16),
                        pltpu.VMEM((LANES, LANES), BF16),
                        pltpu.VMEM((1, LANES), F32),
                        pltpu.VMEM((1, LANES), F32)],
        compiler_params=_params(("arbitrary", "arbitrary")),
        name="plan",
    )(route)


MOE_COMPUTE, MOE_CAST, MOE_ZERO = 1, 2, 4
MOE_PAR_COMPUTE_SHIFT, MOE_PAR_CAST_SHIFT = 3, 4


def _moe_kernel(flag_ref, rb_ref, fe_ref, fc_ref, nu_ref, *refs):
    idx_refs = refs[:MOE_XBUFS]
    u2_hbm, wg_ref, wu_ref, wd_ref, y_ref, wgb, wub, wdb, xbuf, sem = refs[MOE_XBUFS:]
    lead = MOE_XBUFS - 1
    s = pl.program_id(0)
    flags = flag_ref[s]
    n_used = nu_ref[0]
    rb = rb_ref[s]
    slot = rb % MOE_XBUFS

    def start_gather(rows_ref, dst):
        for r in range(MOE_TM):
            pltpu.make_async_copy(u2_hbm.at[pl.ds(rows_ref[0, 0, r], TOK_ROWS)],
                                  xbuf.at[dst, pl.ds(r * TOK_PITCH, TOK_ROWS)], sem.at[dst]).start()

    def wait_gather(dst):
        n_rows = MOE_TM * TOK_ROWS
        pltpu.make_async_copy(u2_hbm.at[pl.ds(0, n_rows)], xbuf.at[dst, pl.ds(0, n_rows)],
                              sem.at[dst]).wait()

    @pl.when(s == 0)
    def _():
        for b in range(lead):
            start_gather(idx_refs[b], b)

    @pl.when((flags & MOE_CAST) != 0)
    def _():
        par = (flags >> MOE_PAR_CAST_SHIFT) & 1
        c = fc_ref[s]
        wgb[par, c] = wg_ref[0].astype(BF16)
        wub[par, c] = wu_ref[0].astype(BF16)
        wdb[par, c] = wd_ref[0].astype(BF16)

    @pl.when((flags & MOE_COMPUTE) != 0)
    def _():
        wait_gather(slot)

        @pl.when(rb + lead < n_used)
        def _():
            start_gather(idx_refs[lead], (rb + lead) % MOE_XBUFS)

        par = (flags >> MOE_PAR_COMPUTE_SHIFT) & 1
        x = _load_token_major(xbuf.at[slot], MOE_TM).astype(BF16)
        acc = jnp.zeros((MOE_TM, D_MODEL), F32)
        for c in range(MOE_CHUNKS):
            hg = jnp.dot(x, wgb[par, c], preferred_element_type=F32)
            hu = jnp.dot(x, wub[par, c], preferred_element_type=F32)
            hid = (_silu(hg) * hu).astype(BF16)
            acc = acc + jnp.dot(hid, wdb[par, c], preferred_element_type=F32)
        _store_token_major(y_ref, acc)

    @pl.when((flags & MOE_ZERO) != 0)
    def _():
        y_ref[...] = jnp.zeros(y_ref.shape, F32)


def _moe(steps, slot_tok3, u2, w_gate, w_up, w_down):
    flags, rb, fetch_e, fetch_c, n_used = steps
    nb = slot_tok3.shape[0]
    d = w_gate.shape[1]
    assert u2.shape[1] == LANES and d == D_MODEL
    tm, fc = MOE_TM, MOE_FC
    smem_idx = lambda fn: pl.BlockSpec((1, 1, tm), fn, memory_space=pltpu.SMEM)
    grid_spec = pltpu.PrefetchScalarGridSpec(
        num_scalar_prefetch=5,
        grid=(flags.shape[0],),
        in_specs=[smem_idx(lambda s, fl, rb, fe, fc_, nu, a=a: (jnp.minimum(rb[s] + a, nb - 1), 0, 0))
                  for a in range(MOE_XBUFS)] + [
                  pl.BlockSpec(memory_space=pl.ANY),
                  pl.BlockSpec((1, d, fc), lambda s, fl, rb, fe, fc_, nu: (fe[s], 0, fc_[s])),
                  pl.BlockSpec((1, d, fc), lambda s, fl, rb, fe, fc_, nu: (fe[s], 0, fc_[s])),
                  pl.BlockSpec((1, fc, d), lambda s, fl, rb, fe, fc_, nu: (fe[s], fc_[s], 0))],
        out_specs=pl.BlockSpec((tm * TOK_ROWS, LANES), lambda s, fl, rb, fe, fc_, nu: (rb[s], 0)),
        scratch_shapes=[pltpu.VMEM((2, MOE_CHUNKS, d, fc), BF16),
                        pltpu.VMEM((2, MOE_CHUNKS, d, fc), BF16),
                        pltpu.VMEM((2, MOE_CHUNKS, fc, d), BF16),
                        pltpu.VMEM((MOE_XBUFS, tm * TOK_PITCH, LANES), F32),
                        pltpu.SemaphoreType.DMA((MOE_XBUFS,))],
    )
    return pl.pallas_call(
        _moe_kernel,
        grid_spec=grid_spec,
        out_shape=jax.ShapeDtypeStruct((nb * tm * TOK_ROWS, LANES), F32),
        compiler_params=pltpu.CompilerParams(dimension_semantics=("arbitrary",),
                                             vmem_limit_bytes=MOE_VMEM_LIMIT),
        name="moe",
    )(flags, rb, fetch_e, fetch_c, n_used, *([slot_tok3] * MOE_XBUFS), u2, w_gate, w_up, w_down)


def _combine_kernel(pos_ref, pos_next_ref, y_hbm, x1_ref, route_ref, gate_ref, g_ref, b_ref,
                    o_ref, ybuf, sem):
    i = pl.program_id(0)
    n_steps = pl.num_programs(0)
    slot = i % 2

    def start_gather(rows_ref, s):
        for r in range(COMB_TM):
            for kk in range(TOP_K):
                pltpu.make_async_copy(y_hbm.at[pl.ds(rows_ref[0, 0, TOP_K * r + kk], TOK_ROWS)],
                                      ybuf.at[s, kk, pl.ds(r * TOK_PITCH, TOK_ROWS)],
                                      sem.at[s]).start(priority=kk)

    @pl.when(i == 0)
    def _():
        start_gather(pos_ref, 0)

    n_rows = COMB_TM * TOK_ROWS
    for kk in range(TOP_K):
        pltpu.make_async_copy(y_hbm.at[pl.ds(0, n_rows)], ybuf.at[slot, kk, pl.ds(0, n_rows)],
                              sem.at[slot]).wait()

    @pl.when(i + 1 < n_steps)
    def _():
        start_gather(pos_next_ref, 1 - slot)

    w1 = route_ref[:, 2:3]
    w2 = route_ref[:, 3:4]
    ffn = (w1 * _load_token_major(ybuf.at[slot, 0], COMB_TM)
           + w2 * _load_token_major(ybuf.at[slot, 1], COMB_TM))
    y = DN_ALPHA * x1_ref[...] + (1.0 + gate_ref[...]) * ffn
    o_ref[...] = _ln(y) * g_ref[...] + b_ref[...]


def _combine(pos3, y, x1, route, gate2, ln2_g, ln2_b, seq):
    n, d = x1.shape
    tm = COMB_TM
    steps = n // tm
    tiles_per_seq = seq // tm
    row = lambda i: (i, 0)
    const = lambda i: (0, 0)
    smem_idx = lambda fn: pl.BlockSpec((1, 1, TOP_K * tm), fn, memory_space=pltpu.SMEM)
    return pl.pallas_call(
        _combine_kernel,
        grid=(steps,),
        in_specs=[smem_idx(lambda i: (i, 0, 0)),
                  smem_idx(lambda i: (jnp.minimum(i + 1, steps - 1), 0, 0)),
                  pl.BlockSpec(memory_space=pl.ANY),
                  pl.BlockSpec((tm, d), row),
                  pl.BlockSpec((tm, LANES), row),
                  pl.BlockSpec((None, 1, d), lambda i: (i // tiles_per_seq, 0, 0)),
                  pl.BlockSpec((1, d), const),
                  pl.BlockSpec((1, d), const)],
        out_specs=pl.BlockSpec((tm, d), row),
        out_shape=jax.ShapeDtypeStruct((n, d), F32),
        scratch_shapes=[pltpu.VMEM((2, TOP_K, tm * TOK_PITCH, LANES), F32),
                        pltpu.SemaphoreType.DMA((2,))],
        compiler_params=_params(("arbitrary",)),
        name="combine",
    )(pos3, pos3, y, x1, route, gate2, ln2_g, ln2_b)


def _retention_tables():
    h = jnp.arange(RET_HEADS, dtype=F32)
    log_gamma = jnp.log1p(-jnp.exp2(-5.0 - h))
    idx = jnp.arange(RET_CHUNK, dtype=F32)
    diff = idx[:, None] - idx[None, :]
    causal = diff >= 0
    dec = jnp.where(causal[None],
                    jnp.exp(jnp.where(causal, diff, 0.0)[None] * log_gamma[:, None, None]), 0.0)
    xi = jnp.exp((idx[None, :] + 1.0) * log_gamma[:, None])
    zeta = jnp.exp((RET_CHUNK - 1.0 - idx[None, :]) * log_gamma[:, None])
    cd = jnp.exp(RET_CHUNK * log_gamma)
    bc = lambda a: jnp.broadcast_to(a[:, :, None], (RET_HEADS, RET_CHUNK, HEAD_DIM))
    return dec, bc(xi), bc(zeta), jnp.broadcast_to(cd[:, None, None], (RET_HEADS, 1, HEAD_DIM))


def _rope_tables():
    half = HEAD_DIM // 2
    inv_freq = jnp.exp(-math.log(ROPE_BASE) * jnp.arange(half, dtype=F32) / half)
    invf = jnp.concatenate([inv_freq, inv_freq])[None, :]
    sgn = jnp.concatenate([-jnp.ones((half,), F32), jnp.ones((half,), F32)])[None, :]
    return invf, sgn


def _step_tables(meta, nb):
    i32 = jnp.int32
    blocks = meta[1, :N_EXPERTS].astype(i32)
    first_block = meta[2, :N_EXPERTS].astype(i32)
    e_iota = jnp.arange(N_EXPERTS, dtype=i32)
    present = blocks > 0
    n_used = jnp.sum(blocks)
    seg_len = jnp.where(present, jnp.maximum(blocks, MOE_CHUNKS), 0)
    seg_end = MOE_CHUNKS + jnp.cumsum(seg_len)
    seg_start = seg_end - seg_len
    s_used = seg_end[-1]
    ordinal = jnp.cumsum(present.astype(i32)) - present.astype(i32)
    later = present[None, :] & (e_iota[None, :] > e_iota[:, None])
    next_e = jnp.min(jnp.where(later, e_iota[None, :], N_EXPERTS), axis=1)
    first_e = jnp.min(jnp.where(present, e_iota, N_EXPERTS))
    last_e = jnp.max(jnp.where(present, e_iota, -1))

    n_steps = MOE_CHUNKS + nb + (MOE_CHUNKS - 1) * (N_EXPERTS - 1)
    s = jnp.arange(n_steps, dtype=i32)
    prologue = s < MOE_CHUNKS
    in_seg = (~prologue) & (s < s_used)
    e_s = jnp.minimum(jnp.sum((seg_end[None, :] <= s[:, None]).astype(i32), axis=1), N_EXPERTS - 1)
    onehot = (e_s[:, None] == e_iota[None, :]).astype(i32)
    pick = lambda t: jnp.sum(onehot * t[None, :], axis=1)
    k = s - pick(seg_start)
    nblk = pick(blocks)
    has_next = pick(next_e) < N_EXPERTS
    compute = in_seg & (k < nblk)
    rb_seg = pick(first_block) + jnp.minimum(k, nblk)
    rb_tail = n_used + (s - s_used)
    rb = jnp.where(prologue, 0, jnp.where(in_seg, rb_seg, rb_tail))
    zero = (s >= s_used) & (rb_tail < nb)
    rb = jnp.clip(rb, 0, nb - 1)
    cast = prologue | (in_seg & has_next & (k < MOE_CHUNKS))
    fetch_e = jnp.where(prologue, first_e,
                        jnp.where(in_seg, jnp.where(has_next, pick(next_e), e_s), last_e))
    fetch_c = jnp.where(prologue, s, jnp.where(in_seg & has_next,
                                               jnp.minimum(k, MOE_CHUNKS - 1), MOE_CHUNKS - 1))
    par_compute = pick(ordinal) % 2
    par_cast = jnp.where(prologue, 0, (pick(ordinal) + 1) % 2)
    flags = (compute.astype(i32) * MOE_COMPUTE + cast.astype(i32) * MOE_CAST
             + zero.astype(i32) * MOE_ZERO + (par_compute << MOE_PAR_COMPUTE_SHIFT)
             + (par_cast << MOE_PAR_CAST_SHIFT))
    return flags, rb, fetch_e, fetch_c, n_used.reshape(1)


def _row_tokens(meta, expert_ids, n_tok, nb):
    counts = meta[0, :N_EXPERTS].astype(jnp.int32)
    blocks = meta[1, :N_EXPERTS].astype(jnp.int32)
    n_slots = n_tok * TOP_K
    fill = blocks * MOE_TM - counts
    e_iota = jnp.arange(N_EXPERTS, dtype=jnp.int32)[:, None]
    r_iota = jnp.arange(MOE_TM, dtype=jnp.int32)[None, :]
    filler_keys = jnp.where(r_iota < fill[:, None], e_iota, N_EXPERTS).reshape(-1)
    order = jnp.argsort(jnp.concatenate([expert_ids.reshape(-1), filler_keys])).astype(jnp.int32)
    slot_tok = jnp.where(order < n_slots, order // TOP_K, 0)
    return (slot_tok * TOK_ROWS).reshape(nb, 1, MOE_TM)


def kernel(x, c, positions, w_ada, b_ada, w_in, conv_w, conv_b, conv_ln_g, conv_ln_b, w_out,
           ln1_g, ln1_b, w_group_router, b_group_router, w_expert_router, b_expert_router,
           w_gate, w_up, w_down, ln2_g, ln2_b):
    batch, seq, d = x.shape
    n_tok = batch * seq
    l = 0
    row = lambda a: a[l][None, :]

    c_pad = jnp.pad(c, ((0, SUBLANES - batch), (0, 0)))
    mod = _ada(c_pad, w_ada[l], b_ada[l][None, :])[:batch]
    shift1, scale1, gate1, shift2, scale2, gate2 = [m[:, None, :] for m in jnp.split(mod, 6, axis=-1)]

    x2 = x.reshape(n_tok, d)
    invf, sgn = _rope_tables()
    h0, q, k, v, gs = _inproj(x2, scale1, shift1, positions.reshape(n_tok, 1), invf, sgn,
                              w_in[l].astype(BF16), seq)

    hc = _conv(h0.reshape(batch, seq, D_CONV), conv_w[l], row(conv_b), row(conv_ln_g),
               row(conv_ln_b)).reshape(n_tok, D_CONV)

    dec, xi, zeta, cd = _retention_tables()
    r = _retention(q, k, v, gs, dec, xi, zeta, cd, batch, seq)

    n_route = N_GROUPS + N_EXPERTS
    w_router = jnp.pad(jnp.concatenate([w_group_router[l], w_expert_router[l]], axis=1),
                       ((0, 0), (0, LANES - n_route)))
    b_router = jnp.pad(jnp.concatenate([b_group_router[l], b_expert_router[l]]),
                       (0, LANES - n_route))[None, :]
    x1, u2, route = _outproj(hc, r, x2, w_out[l].astype(BF16), gate1, row(ln1_g), row(ln1_b),
                             scale2, shift2, w_router, b_router, seq)

    n_slots = n_tok * TOP_K
    nb = n_slots // MOE_TM + N_EXPERTS
    assert N_EXPERTS * (MOE_CHUNKS - 1) * MOE_TM < n_slots
    assert n_slots // MOE_TM >= MOE_XBUFS
    pos_pad, meta = _plan(route)
    slot_tok3 = _row_tokens(meta, route[:, :TOP_K].astype(jnp.int32), n_tok, nb)
    y = _moe(_step_tables(meta, nb), slot_tok3, u2, w_gate[l], w_up[l], w_down[l])

    pos3 = (pos_pad[:, :TOP_K] * TOK_ROWS).reshape(n_tok // COMB_TM, 1, TOP_K * COMB_TM)
    out = _combine(pos3, y, x1, route, gate2, row(ln2_g), row(ln2_b), seq)
    return out.reshape(batch, seq, d)
```

```python
import functools
import math

import jax
import jax.numpy as jnp
from jax import lax
from jax.experimental import pallas as pl
from jax.experimental.pallas import tpu as pltpu

F32 = jnp.float32
BF16 = jnp.bfloat16

D_MODEL = 2048
D_CONV = 1024
D_RET = 1024
CONV_WIDTH = 31
RET_HEADS = 8
HEAD_DIM = 128
RET_CHUNK = 128
ROPE_BASE = 10000.0
N_GROUPS = 4
EXPERTS_PER_GROUP = 8
N_EXPERTS = 32
TOP_K = 2
D_EXPERT = 1024
LN_EPS = 1e-5
DEPTH = 1
DN_ALPHA = (2 * DEPTH) ** 0.25

LANES = 128
SUBLANES = 8
TOK_ROWS = D_MODEL // LANES
TOK_PITCH = TOK_ROWS + SUBLANES
VMEM_LIMIT = 56 * 1024 * 1024
MOE_VMEM_LIMIT = 60 * 1024 * 1024

ADA_TN = 1024
INPROJ_TM = 512
INPROJ_TN = 1024
CONV_TT = 512
CONV_HALO = 32
CONV_ROWS = 128
CONV_LANES = 128
RET_TT = 256
OUTPROJ_TM = 256
PLAN_TB = 1024
MOE_TM = 256
MOE_FC = 256
MOE_CHUNKS = D_EXPERT // MOE_FC
MOE_XBUFS = 3
COMB_TM = 256


def _ln(x):
    mu = jnp.mean(x, axis=-1, keepdims=True)
    xc = x - mu
    var = jnp.mean(xc * xc, axis=-1, keepdims=True)
    return xc * lax.rsqrt(var + LN_EPS)


def _silu(x):
    return x * jax.nn.sigmoid(x)


def _store_token_major(ref, val):
    rows = val.shape[0]
    for s in range(TOK_ROWS):
        ref[pl.ds(s, rows, stride=TOK_ROWS), :] = val[:, s * LANES:(s + 1) * LANES]


def _load_token_major(ref, rows, pitch=TOK_PITCH):
    return jnp.concatenate([ref[pl.ds(s, rows, stride=pitch), :] for s in range(TOK_ROWS)],
                           axis=1)


def _params(sem):
    return pltpu.CompilerParams(dimension_semantics=sem, vmem_limit_bytes=VMEM_LIMIT)


def _ada_kernel(c_ref, w_ref, b_ref, o_ref):
    ca = _silu(c_ref[...]).astype(BF16)
    o_ref[...] = jnp.dot(ca, w_ref[...].astype(BF16), preferred_element_type=F32) + b_ref[...]


def _ada(c_pad, w_ada, b_ada):
    rows, d = c_pad.shape
    n = w_ada.shape[1]
    return pl.pallas_call(
        _ada_kernel,
        grid=(n // ADA_TN,),
        in_specs=[pl.BlockSpec((rows, d), lambda j: (0, 0)),
                  pl.BlockSpec((d, ADA_TN), lambda j: (0, j)),
                  pl.BlockSpec((1, ADA_TN), lambda j: (0, j))],
        out_specs=pl.BlockSpec((rows, ADA_TN), lambda j: (0, j)),
        out_shape=jax.ShapeDtypeStruct((rows, n), F32),
        compiler_params=_params(("arbitrary",)),
        name="ada",
    )(c_pad, w_ada, b_ada)


def _inproj_kernel(x_ref, sc_ref, sh_ref, pos_ref, invf_ref, sgn_ref, w_ref,
                   h0_ref, q_ref, k_ref, v_ref, g_ref,
                   u_scr, cos_scr, sin_scr, acc0, acc1, acc2):
    i = pl.program_id(0)
    j = pl.program_id(1)
    cur = i % 2
    prev = 1 - cur

    def normalise():
        u = _ln(x_ref[...]) * (1.0 + sc_ref[...]) + sh_ref[...]
        u_scr[cur] = u.astype(BF16)
        ang = pos_ref[...].astype(F32) * invf_ref[...]
        cos_scr[cur] = jnp.cos(ang)
        sin_scr[cur] = jnp.sin(ang) * sgn_ref[...]

    def matmul():
        return jnp.dot(u_scr[prev], w_ref[...], preferred_element_type=F32)

    def rotary_to(out_ref, src, scale):
        cos = cos_scr[prev]
        sin = sin_scr[prev]
        for h in range(RET_HEADS):
            xs = src[:, h * HEAD_DIM:(h + 1) * HEAD_DIM]
            rot = xs * cos + pltpu.roll(xs, HEAD_DIM // 2, axis=1) * sin
            if scale is not None:
                rot = rot * scale
            out_ref[:, h * HEAD_DIM:(h + 1) * HEAD_DIM] = rot.astype(out_ref.dtype)

    @pl.when((j == 0) & (i == 0))
    def _():
        normalise()

    @pl.when((j == 0) & (i > 0))
    def _():
        acc0[...] = matmul()
        normalise()

    @pl.when((j == 1) & (i > 0))
    def _():
        acc1[...] = matmul()

    @pl.when((j == 2) & (i > 0))
    def _():
        acc2[...] = matmul()
        h0_ref[...] = acc0[...] * jax.nn.sigmoid(acc1[...])

    @pl.when((j == 3) & (i > 0))
    def _():
        acc0[...] = matmul()
        rotary_to(q_ref, acc2, None)

    @pl.when((j == 4) & (i > 0))
    def _():
        acc1[...] = matmul()
        rotary_to(k_ref, acc0, HEAD_DIM ** -0.5)

    @pl.when((j == 5) & (i > 0))
    def _():
        res = matmul()
        v_ref[...] = acc1[...].astype(BF16)
        g_ref[...] = _silu(res)


def _inproj(x2, scale1, shift1, pos2, invf, sgn, w_in_bf, seq):
    n, d = x2.shape
    tm, tn = INPROJ_TM, INPROJ_TN
    tiles_per_seq = seq // tm
    n_tiles = n // tm
    assert w_in_bf.shape[1] == 6 * tn
    norm_tile = lambda i: jnp.minimum(i, n_tiles - 1)
    row_in = lambda i, j: (norm_tile(i), 0)
    per_batch = lambda i, j: (norm_tile(i) // tiles_per_seq, 0, 0)
    row_out = lambda i, j: (jnp.maximum(i - 1, 0), 0)
    const = lambda i, j: (0, 0)
    out_shape = [jax.ShapeDtypeStruct((n, D_CONV), F32),
                 jax.ShapeDtypeStruct((n, D_RET), BF16),
                 jax.ShapeDtypeStruct((n, D_RET), BF16),
                 jax.ShapeDtypeStruct((n, D_RET), BF16),
                 jax.ShapeDtypeStruct((n, D_RET), F32)]
    return pl.pallas_call(
        _inproj_kernel,
        grid=(n_tiles + 1, 6),
        in_specs=[pl.BlockSpec((tm, d), row_in),
                  pl.BlockSpec((None, 1, d), per_batch),
                  pl.BlockSpec((None, 1, d), per_batch),
                  pl.BlockSpec((tm, 1), row_in),
                  pl.BlockSpec((1, LANES), const),
                  pl.BlockSpec((1, LANES), const),
                  pl.BlockSpec((d, tn), lambda i, j: (0, jnp.where(i > 0, j, 0)))],
        out_specs=[pl.BlockSpec((tm, tn), row_out) for _ in out_shape],
        out_shape=out_shape,
        scratch_shapes=[pltpu.VMEM((2, tm, d), BF16),
                        pltpu.VMEM((2, tm, LANES), F32),
                        pltpu.VMEM((2, tm, LANES), F32),
                        pltpu.VMEM((tm, tn), F32),
                        pltpu.VMEM((tm, tn), F32),
                        pltpu.VMEM((tm, tn), F32)],
        compiler_params=_params(("arbitrary", "arbitrary")),
        name="inproj",
    )(x2, scale1, shift1, pos2, invf, sgn, w_in_bf)


def _conv_kernel(main_ref, halo_ref, w_ref, b_ref, g_ref, beta_ref, o_ref, buf, cbuf):
    t = pl.program_id(1)
    buf[0:CONV_HALO, :] = jnp.where(t > 0, halo_ref[...], 0.0)
    buf[CONV_HALO:, :] = main_ref[...]
    first = CONV_HALO - (CONV_WIDTH - 1)
    rows = CONV_ROWS

    def chunk(ci, carry):
        r0 = pl.multiple_of(ci * rows, rows)
        for lg in range(D_CONV // CONV_LANES):
            lanes = slice(lg * CONV_LANES, (lg + 1) * CONV_LANES)
            acc = None
            for s in range(SUBLANES):
                taps = [j for j in range(CONV_WIDTH) if (first + j) % SUBLANES == s]
                span = rows if s == 0 else rows + SUBLANES
                part = None
                for j in taps:
                    a0 = first + j - s
                    term = w_ref[j:j + 1, lanes] * buf[pl.ds(r0 + a0, span), lanes]
                    part = term if part is None else part + term
                part = part[s:s + rows, :]
                acc = part if acc is None else acc + part
            cbuf[pl.ds(r0, rows), lanes] = acc
        return carry

    lax.fori_loop(0, CONV_TT // rows, chunk, 0)
    y = _ln(cbuf[...] + b_ref[...]) * g_ref[...] + beta_ref[...]
    o_ref[...] = _silu(y).astype(BF16)


def _conv(h0, conv_w, conv_b, ln_g, ln_b):
    b, t, c = h0.shape
    tt = CONV_TT
    ratio = tt // CONV_HALO
    const = lambda bi, ti: (0, 0)
    return pl.pallas_call(
        _conv_kernel,
        grid=(b, t // tt),
        in_specs=[pl.BlockSpec((None, tt, c), lambda bi, ti: (bi, ti, 0)),
                  pl.BlockSpec((None, CONV_HALO, c),
                               lambda bi, ti: (bi, jnp.maximum(ti * ratio - 1, 0), 0)),
                  pl.BlockSpec((CONV_WIDTH, c), const),
                  pl.BlockSpec((1, c), const),
                  pl.BlockSpec((1, c), const),
                  pl.BlockSpec((1, c), const)],
        out_specs=pl.BlockSpec((None, tt, c), lambda bi, ti: (bi, ti, 0)),
        out_shape=jax.ShapeDtypeStruct((b, t, c), BF16),
        scratch_shapes=[pltpu.VMEM((tt + CONV_HALO, c), F32),
                        pltpu.VMEM((tt, c), F32)],
        compiler_params=_params(("arbitrary", "arbitrary")),
        name="conv",
    )(h0, h0, conv_w, conv_b, ln_g, ln_b)


def _ret_kernel(q_ref, k_ref, v_ref, gs_ref, dec_ref, xi_ref, zeta_ref, cd_ref, o_ref, st):
    t = pl.program_id(1)

    @pl.when(t == 0)
    def _():
        st[...] = jnp.zeros(st.shape, F32)

    for c in range(RET_TT // RET_CHUNK):
        rows = slice(c * RET_CHUNK, (c + 1) * RET_CHUNK)
        for h in range(RET_HEADS):
            cols = slice(h * HEAD_DIM, (h + 1) * HEAD_DIM)
            q = q_ref[rows, cols]
            k = k_ref[rows, cols]
            v = v_ref[rows, cols]
            s = lax.dot_general(q, k, (((1,), (1,)), ((), ())), preferred_element_type=F32)
            s = s * dec_ref[h]
            inner = jnp.dot(s.astype(BF16), v, preferred_element_type=F32)
            state = st[h]
            qx = (q.astype(F32) * xi_ref[h]).astype(BF16)
            cross = jnp.dot(qx, state.astype(BF16), preferred_element_type=F32)
            kz = (k.astype(F32) * zeta_ref[h]).astype(BF16)
            kv = lax.dot_general(kz, v, (((0,), (0,)), ((), ())), preferred_element_type=F32)
            st[h] = state * cd_ref[h] + kv
            r = _ln(inner + cross)
            o_ref[rows, cols] = (gs_ref[rows, cols] * r).astype(BF16)


def _retention(q, k, v, gs, dec, xi, zeta, cd, batch, seq):
    n, w = q.shape
    tt = RET_TT
    tiles = seq // tt
    row = lambda bi, ti: (bi * tiles + ti, 0)
    const3 = lambda bi, ti: (0, 0, 0)
    tab = pl.BlockSpec((RET_HEADS, RET_CHUNK, HEAD_DIM), const3)
    return pl.pallas_call(
        _ret_kernel,
        grid=(batch, tiles),
        in_specs=[pl.BlockSpec((tt, w), row)] * 4 + [
            tab, tab, tab, pl.BlockSpec((RET_HEADS, 1, HEAD_DIM), const3)],
        out_specs=pl.BlockSpec((tt, w), row),
        out_shape=jax.ShapeDtypeStruct((n, w), BF16),
        scratch_shapes=[pltpu.VMEM((RET_HEADS, HEAD_DIM, HEAD_DIM), F32)],
        compiler_params=_params(("arbitrary", "arbitrary")),
        name="retention",
    )(q, k, v, gs, dec, xi, zeta, cd)


def _outproj_kernel(hc_ref, r_ref, x_ref, w_ref, gate_ref, g1_ref, b1_ref, sc2_ref, sh2_ref,
                    wr_ref, br_ref, x1_ref, u2_ref, route_ref):
    mix = jnp.dot(hc_ref[...], w_ref[0:D_CONV, :], preferred_element_type=F32)
    mix = mix + jnp.dot(r_ref[...], w_ref[D_CONV:, :], preferred_element_type=F32)
    y = DN_ALPHA * x_ref[...] + (1.0 + gate_ref[...]) * mix
    x1 = _ln(y) * g1_ref[...] + b1_ref[...]
    x1_ref[...] = x1
    u2 = _ln(x1) * (1.0 + sc2_ref[...]) + sh2_ref[...]
    _store_token_major(u2_ref, u2)
    logits = jnp.dot(u2.astype(BF16), wr_ref[...].astype(BF16),
                     preferred_element_type=F32) + br_ref[...]

    lane = lax.broadcasted_iota(jnp.int32, logits.shape, 1).astype(F32)
    neg = jnp.float32(-jnp.inf)
    big = jnp.float32(LANES)
    first_where = lambda m: jnp.min(jnp.where(m, lane, big), axis=-1, keepdims=True)

    is_g = lane < N_GROUPS
    gl = jnp.where(is_g, logits, neg)
    gmax = jnp.max(gl, axis=-1, keepdims=True)
    grp = first_where(gl == gmax)
    gsum = jnp.sum(jnp.where(is_g, jnp.exp(gl - gmax), 0.0), axis=-1, keepdims=True)
    g_w = 1.0 / gsum

    lo = N_GROUPS + grp * EXPERTS_PER_GROUP
    el = jnp.where((lane >= lo) & (lane < lo + EXPERTS_PER_GROUP), logits, neg)
    m1 = jnp.max(el, axis=-1, keepdims=True)
    i1 = first_where(el == m1)
    el2 = jnp.where(lane == i1, neg, el)
    m2 = jnp.max(el2, axis=-1, keepdims=True)
    i2 = first_where(el2 == m2)
    e21 = jnp.exp(m2 - m1)
    w1 = g_w / (1.0 + e21)
    w2 = g_w * e21 / (1.0 + e21)
    route = jnp.where(lane == 0, i1 - N_GROUPS,
                      jnp.where(lane == 1, i2 - N_GROUPS,
                                jnp.where(lane == 2, w1, jnp.where(lane == 3, w2, 0.0))))
    route_ref[...] = route


def _outproj(hc, r, x2, w_out_bf, gate1, ln1_g, ln1_b, scale2, shift2, w_router, b_router, seq):
    n, d = x2.shape
    tm = OUTPROJ_TM
    tiles_per_seq = seq // tm
    row = lambda i: (i, 0)
    per_batch = lambda i: (i // tiles_per_seq, 0, 0)
    const = lambda i: (0, 0)
    vec = pl.BlockSpec((1, d), const)
    mod = pl.BlockSpec((None, 1, d), per_batch)
    return pl.pallas_call(
        _outproj_kernel,
        grid=(n // tm,),
        in_specs=[pl.BlockSpec((tm, D_CONV), row),
                  pl.BlockSpec((tm, D_RET), row),
                  pl.BlockSpec((tm, d), row),
                  pl.BlockSpec((d, d), const),
                  mod, vec, vec, mod, mod,
                  pl.BlockSpec((d, LANES), const),
                  pl.BlockSpec((1, LANES), const)],
        out_specs=[pl.BlockSpec((tm, d), row),
                   pl.BlockSpec((tm * TOK_ROWS, LANES), row),
                   pl.BlockSpec((tm, LANES), row)],
        out_shape=[jax.ShapeDtypeStruct((n, d), F32),
                   jax.ShapeDtypeStruct((n * TOK_ROWS, LANES), F32),
                   jax.ShapeDtypeStruct((n, LANES), F32)],
        compiler_params=_params(("arbitrary",)),
        name="outproj",
    )(hc, r, x2, w_out_bf, gate1, ln1_g, ln1_b, scale2, shift2, w_router, b_router)


def _plan_kernel(route_ref, pos_ref, meta_ref, tri, upper, run, pstart):
    ph = pl.program_id(0)
    i = pl.program_id(1)
    tb = PLAN_TB
    lane = lax.broadcasted_iota(jnp.int32, (tb, LANES), 1).astype(F32)
    e1 = route_ref[:, 0:1]
    e2 = route_ref[:, 1:2]
    member = (lane == e1) | (lane == e2)
    col_count = jnp.sum(member.astype(F32), axis=0, keepdims=True)

    @pl.when((ph == 0) & (i == 0))
    def _():
        r = lax.broadcasted_iota(jnp.int32, (tb, tb), 0)
        c = lax.broadcasted_iota(jnp.int32, (tb, tb), 1)
        tri[...] = (c < r).astype(BF16)
        ru = lax.broadcasted_iota(jnp.int32, (LANES, LANES), 0)
        cu = lax.broadcasted_iota(jnp.int32, (LANES, LANES), 1)
        upper[...] = (ru < cu).astype(BF16)
        run[...] = jnp.zeros(run.shape, F32)

    @pl.when(ph == 0)
    def _():
        run[...] += col_count

    @pl.when((ph == 0) & (i == pl.num_programs(1) - 1))
    def _():
        counts = run[...]
        blocks = jnp.floor((counts + (MOE_TM - 1)) * (1.0 / MOE_TM))
        start_blocks = jnp.dot(jnp.broadcast_to(blocks, (SUBLANES, LANES)).astype(BF16), upper[...],
                               preferred_element_type=F32)[0:1, :]
        pstart[...] = start_blocks * MOE_TM
        row = lax.broadcasted_iota(jnp.int32, (SUBLANES, LANES), 0)
        meta_ref[...] = jnp.where(row == 0, counts, jnp.where(row == 1, blocks,
                                  jnp.where(row == 2, start_blocks, 0.0)))
        run[...] = jnp.zeros(run.shape, F32)

    @pl.when(ph == 1)
    def _():
        earlier = jnp.dot(tri[...], member.astype(BF16), preferred_element_type=F32) + run[...]
        dest = earlier + pstart[...]
        p1 = jnp.sum(jnp.where(lane == e1, dest, 0.0), axis=-1, keepdims=True)
        p2 = jnp.sum(jnp.where(lane == e2, dest, 0.0), axis=-1, keepdims=True)
        pos_ref[...] = jnp.where(lane == 0, p1, jnp.where(lane == 1, p2, 0.0)).astype(jnp.int32)
        run[...] += col_count


def _plan(route):
    n = route.shape[0]
    tb = PLAN_TB
    return pl.pallas_call(
        _plan_kernel,
        grid=(2, n // tb),
        in_specs=[pl.BlockSpec((tb, LANES), lambda ph, i: (i, 0))],
        out_specs=[pl.BlockSpec((tb, LANES), lambda ph, i: (ph * i, 0)),
                   pl.BlockSpec((SUBLANES, LANES), lambda ph, i: (0, 0))],
        out_shape=[jax.ShapeDtypeStruct((n, LANES), jnp.int32),
                   jax.ShapeDtypeStruct((SUBLANES, LANES), F32)],
        scratch_shapes=[pltpu.VMEM((tb, tb), BF16),
                        pltpu.VMEM((LANES, LANES), BF16),
                        pltpu.VMEM((1, LANES), F32),
                        pltpu.VMEM((1, LANES), F32)],
        compiler_params=_params(("arbitrary", "arbitrary")),
        name="plan",
    )(route)


MOE_COMPUTE, MOE_CAST, MOE_ZERO = 1, 2, 4
MOE_PAR_COMPUTE_SHIFT, MOE_PAR_CAST_SHIFT = 3, 4


def _dispatch_kernel(idx_ref, u2_hbm, xs_hbm, sem):
    b = pl.program_id(0)
    slot = b % 2
    n_rows = MOE_TM * TOK_ROWS

    def wait_block(sl):
        pltpu.make_async_copy(u2_hbm.at[pl.ds(0, n_rows)], xs_hbm.at[pl.ds(0, n_rows)],
                              sem.at[sl]).wait()

    base = b * n_rows
    for r in range(MOE_TM):
        pltpu.make_async_copy(u2_hbm.at[pl.ds(idx_ref[0, 0, r], TOK_ROWS)],
                              xs_hbm.at[pl.ds(base + r * TOK_ROWS, TOK_ROWS)], sem.at[slot]).start()

    @pl.when(b > 0)
    def _():
        wait_block(1 - slot)

    @pl.when(b == pl.num_programs(0) - 1)
    def _():
        wait_block(slot)


def _dispatch(slot_tok3, u2):
    nb = slot_tok3.shape[0]
    return pl.pallas_call(
        _dispatch_kernel,
        grid=(nb,),
        in_specs=[pl.BlockSpec((1, 1, MOE_TM), lambda b: (b, 0, 0), memory_space=pltpu.SMEM),
                  pl.BlockSpec(memory_space=pl.ANY)],
        out_specs=pl.BlockSpec(memory_space=pl.ANY),
        out_shape=jax.ShapeDtypeStruct((nb * MOE_TM * TOK_ROWS, LANES), F32),
        scratch_shapes=[pltpu.SemaphoreType.DMA((2,))],
        compiler_params=_params(("arbitrary",)),
        name="dispatch",
    )(slot_tok3, u2)


def _moe_kernel(flag_ref, rb_ref, fe_ref, fc_ref, nu_ref, x_ref, wg_ref, wu_ref, wd_ref, y_ref,
                wgb, wub, wdb):
    s = pl.program_id(0)
    flags = flag_ref[s]

    @pl.when((flags & MOE_CAST) != 0)
    def _():
        par = (flags >> MOE_PAR_CAST_SHIFT) & 1
        c = fc_ref[s]
        wgb[par, c] = wg_ref[0].astype(BF16)
        wub[par, c] = wu_ref[0].astype(BF16)
        wdb[par, c] = wd_ref[0].astype(BF16)

    @pl.when((flags & MOE_COMPUTE) != 0)
    def _():
        par = (flags >> MOE_PAR_COMPUTE_SHIFT) & 1
        x = _load_token_major(x_ref, MOE_TM, TOK_ROWS).astype(BF16)
        acc = jnp.zeros((MOE_TM, D_MODEL), F32)
        for c in range(MOE_CHUNKS):
            hg = jnp.dot(x, wgb[par, c], preferred_element_type=F32)
            hu = jnp.dot(x, wub[par, c], preferred_element_type=F32)
            hid = (_silu(hg) * hu).astype(BF16)
            acc = acc + jnp.dot(hid, wdb[par, c], preferred_element_type=F32)
        _store_token_major(y_ref, acc)

    @pl.when((flags & MOE_ZERO) != 0)
    def _():
        y_ref[...] = jnp.zeros(y_ref.shape, F32)


def _moe(steps, nb, xs, w_gate, w_up, w_down):
    flags, rb, fetch_e, fetch_c, n_used = steps
    d = w_gate.shape[1]
    assert xs.shape == (nb * MOE_TM * TOK_ROWS, LANES) and d == D_MODEL
    tm, fc = MOE_TM, MOE_FC
    grid_spec = pltpu.PrefetchScalarGridSpec(
        num_scalar_prefetch=5,
        grid=(flags.shape[0],),
        in_specs=[pl.BlockSpec((tm * TOK_ROWS, LANES), lambda s, fl, rb, fe, fc_, nu: (rb[s], 0)),
                  pl.BlockSpec((1, d, fc), lambda s, fl, rb, fe, fc_, nu: (fe[s], 0, fc_[s])),
                  pl.BlockSpec((1, d, fc), lambda s, fl, rb, fe, fc_, nu: (fe[s], 0, fc_[s])),
                  pl.BlockSpec((1, fc, d), lambda s, fl, rb, fe, fc_, nu: (fe[s], fc_[s], 0))],
        out_specs=pl.BlockSpec((tm * TOK_ROWS, LANES), lambda s, fl, rb, fe, fc_, nu: (rb[s], 0)),
        scratch_shapes=[pltpu.VMEM((2, MOE_CHUNKS, d, fc), BF16),
                        pltpu.VMEM((2, MOE_CHUNKS, d, fc), BF16),
                        pltpu.VMEM((2, MOE_CHUNKS, fc, d), BF16)],
    )
    return pl.pallas_call(
        _moe_kernel,
        grid_spec=grid_spec,
        out_shape=jax.ShapeDtypeStruct((nb * tm * TOK_ROWS, LANES), F32),
        compiler_params=pltpu.CompilerParams(dimension_semantics=("arbitrary",),
                                             vmem_limit_bytes=MOE_VMEM_LIMIT),
        name="moe",
    )(flags, rb, fetch_e, fetch_c, n_used, xs, w_gate, w_up, w_down)


def _combine_kernel(pos_ref, pos_next_ref, y_hbm, x1_ref, route_ref, gate_ref, g_ref, b_ref,
                    o_ref, ybuf, sem):
    i = pl.program_id(0)
    n_steps = pl.num_programs(0)
    slot = i % 2

    def start_gather(rows_ref, s):
        for r in range(COMB_TM):
            for kk in range(TOP_K):
                pltpu.make_async_copy(y_hbm.at[pl.ds(rows_ref[0, 0, TOP_K * r + kk], TOK_ROWS)],
                                      ybuf.at[s, kk, pl.ds(r * TOK_PITCH, TOK_ROWS)],
                                      sem.at[s]).start(priority=kk)

    @pl.when(i == 0)
    def _():
        start_gather(pos_ref, 0)

    n_rows = COMB_TM * TOK_ROWS
    for kk in range(TOP_K):
        pltpu.make_async_copy(y_hbm.at[pl.ds(0, n_rows)], ybuf.at[slot, kk, pl.ds(0, n_rows)],
                              sem.at[slot]).wait()

    @pl.when(i + 1 < n_steps)
    def _():
        start_gather(pos_next_ref, 1 - slot)

    w1 = route_ref[:, 2:3]
    w2 = route_ref[:, 3:4]
    ffn = (w1 * _load_token_major(ybuf.at[slot, 0], COMB_TM)
           + w2 * _load_token_major(ybuf.at[slot, 1], COMB_TM))
    y = DN_ALPHA * x1_ref[...] + (1.0 + gate_ref[...]) * ffn
    o_ref[...] = _ln(y) * g_ref[...] + b_ref[...]


def _combine(pos3, y, x1, route, gate2, ln2_g, ln2_b, seq):
    n, d = x1.shape
    tm = COMB_TM
    steps = n // tm
    tiles_per_seq = seq // tm
    row = lambda i: (i, 0)
    const = lambda i: (0, 0)
    smem_idx = lambda fn: pl.BlockSpec((1, 1, TOP_K * tm), fn, memory_space=pltpu.SMEM)
    return pl.pallas_call(
        _combine_kernel,
        grid=(steps,),
        in_specs=[smem_idx(lambda i: (i, 0, 0)),
                  smem_idx(lambda i: (jnp.minimum(i + 1, steps - 1), 0, 0)),
                  pl.BlockSpec(memory_space=pl.ANY),
                  pl.BlockSpec((tm, d), row),
                  pl.BlockSpec((tm, LANES), row),
                  pl.BlockSpec((None, 1, d), lambda i: (i // tiles_per_seq, 0, 0)),
                  pl.BlockSpec((1, d), const),
                  pl.BlockSpec((1, d), const)],
        out_specs=pl.BlockSpec((tm, d), row),
        out_shape=jax.ShapeDtypeStruct((n, d), F32),
        scratch_shapes=[pltpu.VMEM((2, TOP_K, tm * TOK_PITCH, LANES), F32),
                        pltpu.SemaphoreType.DMA((2,))],
        compiler_params=_params(("arbitrary",)),
        name="combine",
    )(pos3, pos3, y, x1, route, gate2, ln2_g, ln2_b)


def _retention_tables():
    h = jnp.arange(RET_HEADS, dtype=F32)
    log_gamma = jnp.log1p(-jnp.exp2(-5.0 - h))
    idx = jnp.arange(RET_CHUNK, dtype=F32)
    diff = idx[:, None] - idx[None, :]
    causal = diff >= 0
    dec = jnp.where(causal[None],
                    jnp.exp(jnp.where(causal, diff, 0.0)[None] * log_gamma[:, None, None]), 0.0)
    xi = jnp.exp((idx[None, :] + 1.0) * log_gamma[:, None])
    zeta = jnp.exp((RET_CHUNK - 1.0 - idx[None, :]) * log_gamma[:, None])
    cd = jnp.exp(RET_CHUNK * log_gamma)
    bc = lambda a: jnp.broadcast_to(a[:, :, None], (RET_HEADS, RET_CHUNK, HEAD_DIM))
    return dec, bc(xi), bc(zeta), jnp.broadcast_to(cd[:, None, None], (RET_HEADS, 1, HEAD_DIM))


def _rope_tables():
    half = HEAD_DIM // 2
    inv_freq = jnp.exp(-math.log(ROPE_BASE) * jnp.arange(half, dtype=F32) / half)
    invf = jnp.concatenate([inv_freq, inv_freq])[None, :]
    sgn = jnp.concatenate([-jnp.ones((half,), F32), jnp.ones((half,), F32)])[None, :]
    return invf, sgn


def _step_tables(meta, nb):
    i32 = jnp.int32
    blocks = meta[1, :N_EXPERTS].astype(i32)
    first_block = meta[2, :N_EXPERTS].astype(i32)
    e_iota = jnp.arange(N_EXPERTS, dtype=i32)
    present = blocks > 0
    n_used = jnp.sum(blocks)
    seg_len = jnp.where(present, jnp.maximum(blocks, MOE_CHUNKS), 0)
    seg_end = MOE_CHUNKS + jnp.cumsum(seg_len)
    seg_start = seg_end - seg_len
    s_used = seg_end[-1]
    ordinal = jnp.cumsum(present.astype(i32)) - present.astype(i32)
    later = present[None, :] & (e_iota[None, :] > e_iota[:, None])
    next_e = jnp.min(jnp.where(later, e_iota[None, :], N_EXPERTS), axis=1)
    first_e = jnp.min(jnp.where(present, e_iota, N_EXPERTS))
    last_e = jnp.max(jnp.where(present, e_iota, -1))

    n_steps = MOE_CHUNKS + nb + (MOE_CHUNKS - 1) * (N_EXPERTS - 1)
    s = jnp.arange(n_steps, dtype=i32)
    prologue = s < MOE_CHUNKS
    in_seg = (~prologue) & (s < s_used)
    e_s = jnp.minimum(jnp.sum((seg_end[None, :] <= s[:, None]).astype(i32), axis=1), N_EXPERTS - 1)
    onehot = (e_s[:, None] == e_iota[None, :]).astype(i32)
    pick = lambda t: jnp.sum(onehot * t[None, :], axis=1)
    k = s - pick(seg_start)
    nblk = pick(blocks)
    has_next = pick(next_e) < N_EXPERTS
    compute = in_seg & (k < nblk)
    rb_seg = pick(first_block) + jnp.minimum(k, nblk)
    rb_tail = n_used + (s - s_used)
    rb = jnp.where(prologue, 0, jnp.where(in_seg, rb_seg, rb_tail))
    zero = (s >= s_used) & (rb_tail < nb)
    rb = jnp.clip(rb, 0, nb - 1)
    cast = prologue | (in_seg & has_next & (k < MOE_CHUNKS))
    fetch_e = jnp.where(prologue, first_e,
                        jnp.where(in_seg, jnp.where(has_next, pick(next_e), e_s), last_e))
    fetch_c = jnp.where(prologue, s, jnp.where(in_seg & has_next,
                                               jnp.minimum(k, MOE_CHUNKS - 1), MOE_CHUNKS - 1))
    par_compute = pick(ordinal) % 2
    par_cast = jnp.where(prologue, 0, (pick(ordinal) + 1) % 2)
    flags = (compute.astype(i32) * MOE_COMPUTE + cast.astype(i32) * MOE_CAST
             + zero.astype(i32) * MOE_ZERO + (par_compute << MOE_PAR_COMPUTE_SHIFT)
             + (par_cast << MOE_PAR_CAST_SHIFT))
    return flags, rb, fetch_e, fetch_c, n_used.reshape(1)


def _row_tokens(meta, expert_ids, n_tok, nb):
    counts = meta[0, :N_EXPERTS].astype(jnp.int32)
    blocks = meta[1, :N_EXPERTS].astype(jnp.int32)
    n_slots = n_tok * TOP_K
    fill = blocks * MOE_TM - counts
    e_iota = jnp.arange(N_EXPERTS, dtype=jnp.int32)[:, None]
    r_iota = jnp.arange(MOE_TM, dtype=jnp.int32)[None, :]
    filler_keys = jnp.where(r_iota < fill[:, None], e_iota, N_EXPERTS).reshape(-1)
    order = jnp.argsort(jnp.concatenate([expert_ids.reshape(-1), filler_keys])).astype(jnp.int32)
    slot_tok = jnp.where(order < n_slots, order // TOP_K, 0)
    return (slot_tok * TOK_ROWS).reshape(nb, 1, MOE_TM)


def kernel(x, c, positions, w_ada, b_ada, w_in, conv_w, conv_b, conv_ln_g, conv_ln_b, w_out,
           ln1_g, ln1_b, w_group_router, b_group_router, w_expert_router, b_expert_router,
           w_gate, w_up, w_down, ln2_g, ln2_b):
    batch, seq, d = x.shape
    n_tok = batch * seq
    l = 0
    row = lambda a: a[l][None, :]

    c_pad = jnp.pad(c, ((0, SUBLANES - batch), (0, 0)))
    mod = _ada(c_pad, w_ada[l], b_ada[l][None, :])[:batch]
    shift1, scale1, gate1, shift2, scale2, gate2 = [m[:, None, :] for m in jnp.split(mod, 6, axis=-1)]

    x2 = x.reshape(n_tok, d)
    invf, sgn = _rope_tables()
    h0, q, k, v, gs = _inproj(x2, scale1, shift1, positions.reshape(n_tok, 1), invf, sgn,
                              w_in[l].astype(BF16), seq)

    hc = _conv(h0.reshape(batch, seq, D_CONV), conv_w[l], row(conv_b), row(conv_ln_g),
               row(conv_ln_b)).reshape(n_tok, D_CONV)

    dec, xi, zeta, cd = _retention_tables()
    r = _retention(q, k, v, gs, dec, xi, zeta, cd, batch, seq)

    n_route = N_GROUPS + N_EXPERTS
    w_router = jnp.pad(jnp.concatenate([w_group_router[l], w_expert_router[l]], axis=1),
                       ((0, 0), (0, LANES - n_route)))
    b_router = jnp.pad(jnp.concatenate([b_group_router[l], b_expert_router[l]]),
                       (0, LANES - n_route))[None, :]
    x1, u2, route = _outproj(hc, r, x2, w_out[l].astype(BF16), gate1, row(ln1_g), row(ln1_b),
                             scale2, shift2, w_router, b_router, seq)

    n_slots = n_tok * TOP_K
    nb = n_slots // MOE_TM + N_EXPERTS
    assert N_EXPERTS * (MOE_CHUNKS - 1) * MOE_TM < n_slots
    assert n_slots // MOE_TM >= MOE_XBUFS
    pos_pad, meta = _plan(route)
    slot_tok3 = _row_tokens(meta, route[:, :TOP_K].astype(jnp.int32), n_tok, nb)
    xs = _dispatch(slot_tok3, u2)
    y = _moe(_step_tables(meta, nb), nb, xs, w_gate[l], w_up[l], w_down[l])

    pos3 = (pos_pad[:, :TOP_K] * TOK_ROWS).reshape(n_tok // COMB_TM, 1, TOP_K * COMB_TM)
    out = _combine(pos3, y, x1, route, gate2, row(ln2_g), row(ln2_b), seq)
    return out.reshape(batch, seq, d)
```

```python
import functools
import math

import jax
import jax.numpy as jnp
from jax import lax
from jax.experimental import pallas as pl
from jax.experimental.pallas import tpu as pltpu

F32 = jnp.float32
BF16 = jnp.bfloat16

D_MODEL = 2048
D_CONV = 1024
D_RET = 1024
CONV_WIDTH = 31
RET_HEADS = 8
HEAD_DIM = 128
RET_CHUNK = 128
ROPE_BASE = 10000.0
N_GROUPS = 4
EXPERTS_PER_GROUP = 8
N_EXPERTS = 32
TOP_K = 2
D_EXPERT = 1024
LN_EPS = 1e-5
DEPTH = 1
DN_ALPHA = (2 * DEPTH) ** 0.25

LANES = 128
SUBLANES = 8
TOK_ROWS = D_MODEL // LANES
TOK_PITCH = TOK_ROWS + SUBLANES
VMEM_LIMIT = 56 * 1024 * 1024
MOE_VMEM_LIMIT = 60 * 1024 * 1024

ADA_TN = 1024
INPROJ_TM = 512
INPROJ_TN = 1024
CONV_TT = 512
CONV_HALO = 32
CONV_ROWS = 128
CONV_LANES = 128
RET_TT = 256
OUTPROJ_TM = 256
PLAN_TB = 1024
MOE_TM = 256
MOE_FC = 256
MOE_CHUNKS = D_EXPERT // MOE_FC
MOE_XBUFS = 3
COMB_TM = 256


def _ln(x):
    mu = jnp.mean(x, axis=-1, keepdims=True)
    xc = x - mu
    var = jnp.mean(xc * xc, axis=-1, keepdims=True)
    return xc * lax.rsqrt(var + LN_EPS)


def _silu(x):
    return x * jax.nn.sigmoid(x)


def _store_token_major(ref, val):
    rows = val.shape[0]
    for s in range(TOK_ROWS):
        ref[pl.ds(s, rows, stride=TOK_ROWS), :] = val[:, s * LANES:(s + 1) * LANES]


def _load_token_major(ref, rows):
    return jnp.concatenate([ref[pl.ds(s, rows, stride=TOK_PITCH), :] for s in range(TOK_ROWS)],
                           axis=1)


def _params(sem):
    return pltpu.CompilerParams(dimension_semantics=sem, vmem_limit_bytes=VMEM_LIMIT)


def _ada_kernel(c_ref, w_ref, b_ref, o_ref):
    ca = _silu(c_ref[...]).astype(BF16)
    o_ref[...] = jnp.dot(ca, w_ref[...].astype(BF16), preferred_element_type=F32) + b_ref[...]


def _ada(c_pad, w_ada, b_ada):
    rows, d = c_pad.shape
    n = w_ada.shape[1]
    return pl.pallas_call(
        _ada_kernel,
        grid=(n // ADA_TN,),
        in_specs=[pl.BlockSpec((rows, d), lambda j: (0, 0)),
                  pl.BlockSpec((d, ADA_TN), lambda j: (0, j)),
                  pl.BlockSpec((1, ADA_TN), lambda j: (0, j))],
        out_specs=pl.BlockSpec((rows, ADA_TN), lambda j: (0, j)),
        out_shape=jax.ShapeDtypeStruct((rows, n), F32),
        compiler_params=_params(("arbitrary",)),
        name="ada",
    )(c_pad, w_ada, b_ada)


def _inproj_kernel(x_ref, sc_ref, sh_ref, pos_ref, invf_ref, sgn_ref, w_ref,
                   h0_ref, q_ref, k_ref, v_ref, g_ref,
                   u_scr, cos_scr, sin_scr, acc0, acc1, acc2):
    i = pl.program_id(0)
    j = pl.program_id(1)
    cur = i % 2
    prev = 1 - cur

    def normalise():
        u = _ln(x_ref[...]) * (1.0 + sc_ref[...]) + sh_ref[...]
        u_scr[cur] = u.astype(BF16)
        ang = pos_ref[...].astype(F32) * invf_ref[...]
        cos_scr[cur] = jnp.cos(ang)
        sin_scr[cur] = jnp.sin(ang) * sgn_ref[...]

    def matmul():
        return jnp.dot(u_scr[prev], w_ref[...], preferred_element_type=F32)

    def rotary_to(out_ref, src, scale):
        cos = cos_scr[prev]
        sin = sin_scr[prev]
        for h in range(RET_HEADS):
            xs = src[:, h * HEAD_DIM:(h + 1) * HEAD_DIM]
            rot = xs * cos + pltpu.roll(xs, HEAD_DIM // 2, axis=1) * sin
            if scale is not None:
                rot = rot * scale
            out_ref[:, h * HEAD_DIM:(h + 1) * HEAD_DIM] = rot.astype(out_ref.dtype)

    @pl.when((j == 0) & (i == 0))
    def _():
        normalise()

    @pl.when((j == 0) & (i > 0))
    def _():
        acc0[...] = matmul()
        normalise()

    @pl.when((j == 1) & (i > 0))
    def _():
        acc1[...] = matmul()

    @pl.when((j == 2) & (i > 0))
    def _():
        acc2[...] = matmul()
        h0_ref[...] = acc0[...] * jax.nn.sigmoid(acc1[...])

    @pl.when((j == 3) & (i > 0))
    def _():
        acc0[...] = matmul()
        rotary_to(q_ref, acc2, None)

    @pl.when((j == 4) & (i > 0))
    def _():
        acc1[...] = matmul()
        rotary_to(k_ref, acc0, HEAD_DIM ** -0.5)

    @pl.when((j == 5) & (i > 0))
    def _():
        res = matmul()
        v_ref[...] = acc1[...].astype(BF16)
        g_ref[...] = _silu(res)


def _inproj(x2, scale1, shift1, pos2, invf, sgn, w_in_bf, seq):
    n, d = x2.shape
    tm, tn = INPROJ_TM, INPROJ_TN
    tiles_per_seq = seq // tm
    n_tiles = n // tm
    assert w_in_bf.shape[1] == 6 * tn
    norm_tile = lambda i: jnp.minimum(i, n_tiles - 1)
    row_in = lambda i, j: (norm_tile(i), 0)
    per_batch = lambda i, j: (norm_tile(i) // tiles_per_seq, 0, 0)
    row_out = lambda i, j: (jnp.maximum(i - 1, 0), 0)
    const = lambda i, j: (0, 0)
    out_shape = [jax.ShapeDtypeStruct((n, D_CONV), F32),
                 jax.ShapeDtypeStruct((n, D_RET), BF16),
                 jax.ShapeDtypeStruct((n, D_RET), BF16),
                 jax.ShapeDtypeStruct((n, D_RET), BF16),
                 jax.ShapeDtypeStruct((n, D_RET), F32)]
    return pl.pallas_call(
        _inproj_kernel,
        grid=(n_tiles + 1, 6),
        in_specs=[pl.BlockSpec((tm, d), row_in),
                  pl.BlockSpec((None, 1, d), per_batch),
                  pl.BlockSpec((None, 1, d), per_batch),
                  pl.BlockSpec((tm, 1), row_in),
                  pl.BlockSpec((1, LANES), const),
                  pl.BlockSpec((1, LANES), const),
                  pl.BlockSpec((d, tn), lambda i, j: (0, jnp.where(i > 0, j, 0)))],
        out_specs=[pl.BlockSpec((tm, tn), row_out) for _ in out_shape],
        out_shape=out_shape,
        scratch_shapes=[pltpu.VMEM((2, tm, d), BF16),
                        pltpu.VMEM((2, tm, LANES), F32),
                        pltpu.VMEM((2, tm, LANES), F32),
                        pltpu.VMEM((tm, tn), F32),
                        pltpu.VMEM((tm, tn), F32),
                        pltpu.VMEM((tm, tn), F32)],
        compiler_params=_params(("arbitrary", "arbitrary")),
        name="inproj",
    )(x2, scale1, shift1, pos2, invf, sgn, w_in_bf)


def _conv_kernel(main_ref, halo_ref, w_ref, b_ref, g_ref, beta_ref, o_ref, buf, cbuf):
    t = pl.program_id(1)
    buf[0:CONV_HALO, :] = jnp.where(t > 0, halo_ref[...], 0.0)
    buf[CONV_HALO:, :] = main_ref[...]
    first = CONV_HALO - (CONV_WIDTH - 1)
    rows = CONV_ROWS

    def chunk(ci, carry):
        r0 = pl.multiple_of(ci * rows, rows)
        for lg in range(D_CONV // CONV_LANES):
            lanes = slice(lg * CONV_LANES, (lg + 1) * CONV_LANES)
            acc = None
            for s in range(SUBLANES):
                taps = [j for j in range(CONV_WIDTH) if (first + j) % SUBLANES == s]
                span = rows if s == 0 else rows + SUBLANES
                part = None
                for j in taps:
                    a0 = first + j - s
                    term = w_ref[j:j + 1, lanes] * buf[pl.ds(r0 + a0, span), lanes]
                    part = term if part is None else part + term
                part = part[s:s + rows, :]
                acc = part if acc is None else acc + part
            cbuf[pl.ds(r0, rows), lanes] = acc
        return carry

    lax.fori_loop(0, CONV_TT // rows, chunk, 0)
    y = _ln(cbuf[...] + b_ref[...]) * g_ref[...] + beta_ref[...]
    o_ref[...] = _silu(y).astype(BF16)


def _conv(h0, conv_w, conv_b, ln_g, ln_b):
    b, t, c = h0.shape
    tt = CONV_TT
    ratio = tt // CONV_HALO
    const = lambda bi, ti: (0, 0)
    return pl.pallas_call(
        _conv_kernel,
        grid=(b, t // tt),
        in_specs=[pl.BlockSpec((None, tt, c), lambda bi, ti: (bi, ti, 0)),
                  pl.BlockSpec((None, CONV_HALO, c),
                               lambda bi, ti: (bi, jnp.maximum(ti * ratio - 1, 0), 0)),
                  pl.BlockSpec((CONV_WIDTH, c), const),
                  pl.BlockSpec((1, c), const),
                  pl.BlockSpec((1, c), const),
                  pl.BlockSpec((1, c), const)],
        out_specs=pl.BlockSpec((None, tt, c), lambda bi, ti: (bi, ti, 0)),
        out_shape=jax.ShapeDtypeStruct((b, t, c), BF16),
        scratch_shapes=[pltpu.VMEM((tt + CONV_HALO, c), F32),
                        pltpu.VMEM((tt, c), F32)],
        compiler_params=_params(("arbitrary", "arbitrary")),
        name="conv",
    )(h0, h0, conv_w, conv_b, ln_g, ln_b)


def _ret_kernel(q_ref, k_ref, v_ref, gs_ref, dec_ref, xi_ref, zeta_ref, cd_ref, o_ref, st):
    t = pl.program_id(1)

    @pl.when(t == 0)
    def _():
        st[...] = jnp.zeros(st.shape, F32)

    for c in range(RET_TT // RET_CHUNK):
        rows = slice(c * RET_CHUNK, (c + 1) * RET_CHUNK)
        for h in range(RET_HEADS):
            cols = slice(h * HEAD_DIM, (h + 1) * HEAD_DIM)
            q = q_ref[rows, cols]
            k = k_ref[rows, cols]
            v = v_ref[rows, cols]
            s = lax.dot_general(q, k, (((1,), (1,)), ((), ())), preferred_element_type=F32)
            s = s * dec_ref[h]
            inner = jnp.dot(s.astype(BF16), v, preferred_element_type=F32)
            state = st[h]
            qx = (q.astype(F32) * xi_ref[h]).astype(BF16)
            cross = jnp.dot(qx, state.astype(BF16), preferred_element_type=F32)
            kz = (k.astype(F32) * zeta_ref[h]).astype(BF16)
            kv = lax.dot_general(kz, v, (((0,), (0,)), ((), ())), preferred_element_type=F32)
            st[h] = state * cd_ref[h] + kv
            r = _ln(inner + cross)
            o_ref[rows, cols] = (gs_ref[rows, cols] * r).astype(BF16)


def _retention(q, k, v, gs, dec, xi, zeta, cd, batch, seq):
    n, w = q.shape
    tt = RET_TT
    tiles = seq // tt
    row = lambda bi, ti: (bi * tiles + ti, 0)
    const3 = lambda bi, ti: (0, 0, 0)
    tab = pl.BlockSpec((RET_HEADS, RET_CHUNK, HEAD_DIM), const3)
    return pl.pallas_call(
        _ret_kernel,
        grid=(batch, tiles),
        in_specs=[pl.BlockSpec((tt, w), row)] * 4 + [
            tab, tab, tab, pl.BlockSpec((RET_HEADS, 1, HEAD_DIM), const3)],
        out_specs=pl.BlockSpec((tt, w), row),
        out_shape=jax.ShapeDtypeStruct((n, w), BF16),
        scratch_shapes=[pltpu.VMEM((RET_HEADS, HEAD_DIM, HEAD_DIM), F32)],
        compiler_params=_params(("arbitrary", "arbitrary")),
        name="retention",
    )(q, k, v, gs, dec, xi, zeta, cd)


def _outproj_kernel(hc_ref, r_ref, x_ref, w_ref, gate_ref, g1_ref, b1_ref, sc2_ref, sh2_ref,
                    wr_ref, br_ref, x1_ref, u2_ref, route_ref, counts_ref, cnt):
    mix = jnp.dot(hc_ref[...], w_ref[0:D_CONV, :], preferred_element_type=F32)
    mix = mix + jnp.dot(r_ref[...], w_ref[D_CONV:, :], preferred_element_type=F32)
    y = DN_ALPHA * x_ref[...] + (1.0 + gate_ref[...]) * mix
    x1 = _ln(y) * g1_ref[...] + b1_ref[...]
    x1_ref[...] = x1
    u2 = _ln(x1) * (1.0 + sc2_ref[...]) + sh2_ref[...]
    _store_token_major(u2_ref, u2)
    logits = jnp.dot(u2.astype(BF16), wr_ref[...].astype(BF16),
                     preferred_element_type=F32) + br_ref[...]

    lane = lax.broadcasted_iota(jnp.int32, logits.shape, 1).astype(F32)
    neg = jnp.float32(-jnp.inf)
    big = jnp.float32(LANES)
    first_where = lambda m: jnp.min(jnp.where(m, lane, big), axis=-1, keepdims=True)

    is_g = lane < N_GROUPS
    gl = jnp.where(is_g, logits, neg)
    gmax = jnp.max(gl, axis=-1, keepdims=True)
    grp = first_where(gl == gmax)
    gsum = jnp.sum(jnp.where(is_g, jnp.exp(gl - gmax), 0.0), axis=-1, keepdims=True)
    g_w = 1.0 / gsum

    lo = N_GROUPS + grp * EXPERTS_PER_GROUP
    el = jnp.where((lane >= lo) & (lane < lo + EXPERTS_PER_GROUP), logits, neg)
    m1 = jnp.max(el, axis=-1, keepdims=True)
    i1 = first_where(el == m1)
    el2 = jnp.where(lane == i1, neg, el)
    m2 = jnp.max(el2, axis=-1, keepdims=True)
    i2 = first_where(el2 == m2)
    e21 = jnp.exp(m2 - m1)
    w1 = g_w / (1.0 + e21)
    w2 = g_w * e21 / (1.0 + e21)
    route = jnp.where(lane == 0, i1 - N_GROUPS,
                      jnp.where(lane == 1, i2 - N_GROUPS,
                                jnp.where(lane == 2, w1, jnp.where(lane == 3, w2, 0.0))))
    route_ref[...] = route

    @pl.when(pl.program_id(0) == 0)
    def _():
        cnt[...] = jnp.zeros(cnt.shape, F32)

    member = (lane == i1 - N_GROUPS) | (lane == i2 - N_GROUPS)
    cnt[...] += jnp.sum(member.astype(F32), axis=0, keepdims=True)
    counts_ref[...] = jnp.broadcast_to(cnt[...], counts_ref.shape)


def _outproj(hc, r, x2, w_out_bf, gate1, ln1_g, ln1_b, scale2, shift2, w_router, b_router, seq):
    n, d = x2.shape
    tm = OUTPROJ_TM
    tiles_per_seq = seq // tm
    row = lambda i: (i, 0)
    per_batch = lambda i: (i // tiles_per_seq, 0, 0)
    const = lambda i: (0, 0)
    vec = pl.BlockSpec((1, d), const)
    mod = pl.BlockSpec((None, 1, d), per_batch)
    return pl.pallas_call(
        _outproj_kernel,
        grid=(n // tm,),
        in_specs=[pl.BlockSpec((tm, D_CONV), row),
                  pl.BlockSpec((tm, D_RET), row),
                  pl.BlockSpec((tm, d), row),
                  pl.BlockSpec((d, d), const),
                  mod, vec, vec, mod, mod,
                  pl.BlockSpec((d, LANES), const),
                  pl.BlockSpec((1, LANES), const)],
        out_specs=[pl.BlockSpec((tm, d), row),
                   pl.BlockSpec((tm * TOK_ROWS, LANES), row),
                   pl.BlockSpec((tm, LANES), row),
                   pl.BlockSpec((SUBLANES, LANES), const)],
        out_shape=[jax.ShapeDtypeStruct((n, d), F32),
                   jax.ShapeDtypeStruct((n * TOK_ROWS, LANES), F32),
                   jax.ShapeDtypeStruct((n, LANES), F32),
                   jax.ShapeDtypeStruct((SUBLANES, LANES), F32)],
        scratch_shapes=[pltpu.VMEM((1, LANES), F32)],
        compiler_params=_params(("arbitrary",)),
        name="outproj",
    )(hc, r, x2, w_out_bf, gate1, ln1_g, ln1_b, scale2, shift2, w_router, b_router)


def _plan_kernel(route_ref, counts_ref, pos_ref, meta_ref, tri, run, pstart):
    i = pl.program_id(0)
    tb = PLAN_TB
    lane = lax.broadcasted_iota(jnp.int32, (tb, LANES), 1).astype(F32)
    e1 = route_ref[:, 0:1]
    e2 = route_ref[:, 1:2]
    member = (lane == e1) | (lane == e2)
    col_count = jnp.sum(member.astype(F32), axis=0, keepdims=True)

    @pl.when(i == 0)
    def _():
        r = lax.broadcasted_iota(jnp.int32, (tb, tb), 0)
        c = lax.broadcasted_iota(jnp.int32, (tb, tb), 1)
        tri[...] = (c < r).astype(BF16)
        ru = lax.broadcasted_iota(jnp.int32, (LANES, LANES), 0)
        cu = lax.broadcasted_iota(jnp.int32, (LANES, LANES), 1)
        upper_tri = (ru < cu).astype(BF16)
        counts = counts_ref[0:1, :]
        blocks = jnp.floor((counts + (MOE_TM - 1)) * (1.0 / MOE_TM))
        start_blocks = jnp.dot(jnp.broadcast_to(blocks, (SUBLANES, LANES)).astype(BF16), upper_tri,
                               preferred_element_type=F32)[0:1, :]
        pstart[...] = start_blocks * MOE_TM
        row = lax.broadcasted_iota(jnp.int32, (SUBLANES, LANES), 0)
        meta_ref[...] = jnp.where(row == 0, counts, jnp.where(row == 1, blocks,
                                  jnp.where(row == 2, start_blocks, 0.0)))
        run[...] = jnp.zeros(run.shape, F32)

    earlier = jnp.dot(tri[...], member.astype(BF16), preferred_element_type=F32) + run[...]
    dest = earlier + pstart[...]
    p1 = jnp.sum(jnp.where(lane == e1, dest, 0.0), axis=-1, keepdims=True)
    p2 = jnp.sum(jnp.where(lane == e2, dest, 0.0), axis=-1, keepdims=True)
    pos_ref[...] = jnp.where(lane == 0, p1, jnp.where(lane == 1, p2, 0.0)).astype(jnp.int32)
    run[...] += col_count


def _plan(route, counts):
    n = route.shape[0]
    tb = PLAN_TB
    return pl.pallas_call(
        _plan_kernel,
        grid=(n // tb,),
        in_specs=[pl.BlockSpec((tb, LANES), lambda i: (i, 0)),
                  pl.BlockSpec((SUBLANES, LANES), lambda i: (0, 0))],
        out_specs=[pl.BlockSpec((tb, LANES), lambda i: (i, 0)),
                   pl.BlockSpec((SUBLANES, LANES), lambda i: (0, 0))],
        out_shape=[jax.ShapeDtypeStruct((n, LANES), jnp.int32),
                   jax.ShapeDtypeStruct((SUBLANES, LANES), F32)],
        scratch_shapes=[pltpu.VMEM((tb, tb), BF16),
                        pltpu.VMEM((1, LANES), F32),
                        pltpu.VMEM((1, LANES), F32)],
        compiler_params=_params(("arbitrary",)),
        name="plan",
    )(route, counts)


MOE_COMPUTE, MOE_CAST, MOE_ZERO = 1, 2, 4
MOE_PAR_COMPUTE_SHIFT, MOE_PAR_CAST_SHIFT = 3, 4


def _moe_kernel(flag_ref, rb_ref, fe_ref, fc_ref, nu_ref, *refs):
    idx_refs = refs[:MOE_XBUFS]
    u2_hbm, wg_ref, wu_ref, wd_ref, y_ref, wgb, wub, wdb, xbuf, sem = refs[MOE_XBUFS:]
    lead = MOE_XBUFS - 1
    s = pl.program_id(0)
    flags = flag_ref[s]
    n_used = nu_ref[0]
    rb = rb_ref[s]
    slot = rb % MOE_XBUFS

    def start_gather(rows_ref, dst):
        for r in range(MOE_TM):
            pltpu.make_async_copy(u2_hbm.at[pl.ds(rows_ref[0, 0, r], TOK_ROWS)],
                                  xbuf.at[dst, pl.ds(r * TOK_PITCH, TOK_ROWS)], sem.at[dst]).start()

    def wait_gather(dst):
        n_rows = MOE_TM * TOK_ROWS
        pltpu.make_async_copy(u2_hbm.at[pl.ds(0, n_rows)], xbuf.at[dst, pl.ds(0, n_rows)],
                              sem.at[dst]).wait()

    @pl.when(s == 0)
    def _():
        for b in range(lead):
            start_gather(idx_refs[b], b)

    @pl.when((flags & MOE_CAST) != 0)
    def _():
        par = (flags >> MOE_PAR_CAST_SHIFT) & 1
        c = fc_ref[s]
        wgb[par, c] = wg_ref[0].astype(BF16)
        wub[par, c] = wu_ref[0].astype(BF16)
        wdb[par, c] = wd_ref[0].astype(BF16)

    @pl.when((flags & MOE_COMPUTE) != 0)
    def _():
        wait_gather(slot)

        @pl.when(rb + lead < n_used)
        def _():
            start_gather(idx_refs[lead], (rb + lead) % MOE_XBUFS)

        par = (flags >> MOE_PAR_COMPUTE_SHIFT) & 1
        x = _load_token_major(xbuf.at[slot], MOE_TM).astype(BF16)
        acc = jnp.zeros((MOE_TM, D_MODEL), F32)
        for c in range(MOE_CHUNKS):
            hg = jnp.dot(x, wgb[par, c], preferred_element_type=F32)
            hu = jnp.dot(x, wub[par, c], preferred_element_type=F32)
            hid = (_silu(hg) * hu).astype(BF16)
            acc = acc + jnp.dot(hid, wdb[par, c], preferred_element_type=F32)
        _store_token_major(y_ref, acc)

    @pl.when((flags & MOE_ZERO) != 0)
    def _():
        y_ref[...] = jnp.zeros(y_ref.shape, F32)


def _moe(steps, slot_tok3, u2, w_gate, w_up, w_down):
    flags, rb, fetch_e, fetch_c, n_used = steps
    nb = slot_tok3.shape[0]
    d = w_gate.shape[1]
    assert u2.shape[1] == LANES and d == D_MODEL
    tm, fc = MOE_TM, MOE_FC
    smem_idx = lambda fn: pl.BlockSpec((1, 1, tm), fn, memory_space=pltpu.SMEM)
    grid_spec = pltpu.PrefetchScalarGridSpec(
        num_scalar_prefetch=5,
        grid=(flags.shape[0],),
        in_specs=[smem_idx(lambda s, fl, rb, fe, fc_, nu, a=a: (jnp.minimum(rb[s] + a, nb - 1), 0, 0))
                  for a in range(MOE_XBUFS)] + [
                  pl.BlockSpec(memory_space=pl.ANY),
                  pl.BlockSpec((1, d, fc), lambda s, fl, rb, fe, fc_, nu: (fe[s], 0, fc_[s])),
                  pl.BlockSpec((1, d, fc), lambda s, fl, rb, fe, fc_, nu: (fe[s], 0, fc_[s])),
                  pl.BlockSpec((1, fc, d), lambda s, fl, rb, fe, fc_, nu: (fe[s], fc_[s], 0))],
        out_specs=pl.BlockSpec((tm * TOK_ROWS, LANES), lambda s, fl, rb, fe, fc_, nu: (rb[s], 0)),
        scratch_shapes=[pltpu.VMEM((2, MOE_CHUNKS, d, fc), BF16),
                        pltpu.VMEM((2, MOE_CHUNKS, d, fc), BF16),
                        pltpu.VMEM((2, MOE_CHUNKS, fc, d), BF16),
                        pltpu.VMEM((MOE_XBUFS, tm * TOK_PITCH, LANES), F32),
                        pltpu.SemaphoreType.DMA((MOE_XBUFS,))],
    )
    return pl.pallas_call(
        _moe_kernel,
        grid_spec=grid_spec,
        out_shape=jax.ShapeDtypeStruct((nb * tm * TOK_ROWS, LANES), F32),
        compiler_params=pltpu.CompilerParams(dimension_semantics=("arbitrary",),
                                             vmem_limit_bytes=MOE_VMEM_LIMIT),
        name="moe",
    )(flags, rb, fetch_e, fetch_c, n_used, *([slot_tok3] * MOE_XBUFS), u2, w_gate, w_up, w_down)


def _combine_kernel(pos_ref, pos_next_ref, y_hbm, x1_ref, route_ref, gate_ref, g_ref, b_ref,
                    o_ref, ybuf, sem):
    i = pl.program_id(0)
    n_steps = pl.num_programs(0)
    slot = i % 2

    def start_gather(rows_ref, s):
        for r in range(COMB_TM):
            for kk in range(TOP_K):
                pltpu.make_async_copy(y_hbm.at[pl.ds(rows_ref[0, 0, TOP_K * r + kk], TOK_ROWS)],
                                      ybuf.at[s, kk, pl.ds(r * TOK_PITCH, TOK_ROWS)],
                                      sem.at[s]).start(priority=kk)

    @pl.when(i == 0)
    def _():
        start_gather(pos_ref, 0)

    n_rows = COMB_TM * TOK_ROWS
    for kk in range(TOP_K):
        pltpu.make_async_copy(y_hbm.at[pl.ds(0, n_rows)], ybuf.at[slot, kk, pl.ds(0, n_rows)],
                              sem.at[slot]).wait()

    @pl.when(i + 1 < n_steps)
    def _():
        start_gather(pos_next_ref, 1 - slot)

    w1 = route_ref[:, 2:3]
    w2 = route_ref[:, 3:4]
    ffn = (w1 * _load_token_major(ybuf.at[slot, 0], COMB_TM)
           + w2 * _load_token_major(ybuf.at[slot, 1], COMB_TM))
    y = DN_ALPHA * x1_ref[...] + (1.0 + gate_ref[...]) * ffn
    o_ref[...] = _ln(y) * g_ref[...] + b_ref[...]


def _combine(pos3, y, x1, route, gate2, ln2_g, ln2_b, seq):
    n, d = x1.shape
    tm = COMB_TM
    steps = n // tm
    tiles_per_seq = seq // tm
    row = lambda i: (i, 0)
    const = lambda i: (0, 0)
    smem_idx = lambda fn: pl.BlockSpec((1, 1, TOP_K * tm), fn, memory_space=pltpu.SMEM)
    return pl.pallas_call(
        _combine_kernel,
        grid=(steps,),
        in_specs=[smem_idx(lambda i: (i, 0, 0)),
                  smem_idx(lambda i: (jnp.minimum(i + 1, steps - 1), 0, 0)),
                  pl.BlockSpec(memory_space=pl.ANY),
                  pl.BlockSpec((tm, d), row),
                  pl.BlockSpec((tm, LANES), row),
                  pl.BlockSpec((None, 1, d), lambda i: (i // tiles_per_seq, 0, 0)),
                  pl.BlockSpec((1, d), const),
                  pl.BlockSpec((1, d), const)],
        out_specs=pl.BlockSpec((tm, d), row),
        out_shape=jax.ShapeDtypeStruct((n, d), F32),
        scratch_shapes=[pltpu.VMEM((2, TOP_K, tm * TOK_PITCH, LANES), F32),
                        pltpu.SemaphoreType.DMA((2,))],
        compiler_params=_params(("arbitrary",)),
        name="combine",
    )(pos3, pos3, y, x1, route, gate2, ln2_g, ln2_b)


def _retention_tables():
    h = jnp.arange(RET_HEADS, dtype=F32)
    log_gamma = jnp.log1p(-jnp.exp2(-5.0 - h))
    idx = jnp.arange(RET_CHUNK, dtype=F32)
    diff = idx[:, None] - idx[None, :]
    causal = diff >= 0
    dec = jnp.where(causal[None],
                    jnp.exp(jnp.where(causal, diff, 0.0)[None] * log_gamma[:, None, None]), 0.0)
    xi = jnp.exp((idx[None, :] + 1.0) * log_gamma[:, None])
    zeta = jnp.exp((RET_CHUNK - 1.0 - idx[None, :]) * log_gamma[:, None])
    cd = jnp.exp(RET_CHUNK * log_gamma)
    bc = lambda a: jnp.broadcast_to(a[:, :, None], (RET_HEADS, RET_CHUNK, HEAD_DIM))
    return dec, bc(xi), bc(zeta), jnp.broadcast_to(cd[:, None, None], (RET_HEADS, 1, HEAD_DIM))


def _rope_tables():
    half = HEAD_DIM // 2
    inv_freq = jnp.exp(-math.log(ROPE_BASE) * jnp.arange(half, dtype=F32) / half)
    invf = jnp.concatenate([inv_freq, inv_freq])[None, :]
    sgn = jnp.concatenate([-jnp.ones((half,), F32), jnp.ones((half,), F32)])[None, :]
    return invf, sgn


def _step_tables(meta, nb):
    i32 = jnp.int32
    blocks = meta[1, :N_EXPERTS].astype(i32)
    first_block = meta[2, :N_EXPERTS].astype(i32)
    e_iota = jnp.arange(N_EXPERTS, dtype=i32)
    present = blocks > 0
    n_used = jnp.sum(blocks)
    seg_len = jnp.where(present, jnp.maximum(blocks, MOE_CHUNKS), 0)
    seg_end = MOE_CHUNKS + jnp.cumsum(seg_len)
    seg_start = seg_end - seg_len
    s_used = seg_end[-1]
    ordinal = jnp.cumsum(present.astype(i32)) - present.astype(i32)
    later = present[None, :] & (e_iota[None, :] > e_iota[:, None])
    next_e = jnp.min(jnp.where(later, e_iota[None, :], N_EXPERTS), axis=1)
    first_e = jnp.min(jnp.where(present, e_iota, N_EXPERTS))
    last_e = jnp.max(jnp.where(present, e_iota, -1))

    n_steps = MOE_CHUNKS + nb + (MOE_CHUNKS - 1) * (N_EXPERTS - 1)
    s = jnp.arange(n_steps, dtype=i32)
    prologue = s < MOE_CHUNKS
    in_seg = (~prologue) & (s < s_used)
    e_s = jnp.minimum(jnp.sum((seg_end[None, :] <= s[:, None]).astype(i32), axis=1), N_EXPERTS - 1)
    onehot = (e_s[:, None] == e_iota[None, :]).astype(i32)
    pick = lambda t: jnp.sum(onehot * t[None, :], axis=1)
    k = s - pick(seg_start)
    nblk = pick(blocks)
    has_next = pick(next_e) < N_EXPERTS
    compute = in_seg & (k < nblk)
    rb_seg = pick(first_block) + jnp.minimum(k, nblk)
    rb_tail = n_used + (s - s_used)
    rb = jnp.where(prologue, 0, jnp.where(in_seg, rb_seg, rb_tail))
    zero = (s >= s_used) & (rb_tail < nb)
    rb = jnp.clip(rb, 0, nb - 1)
    cast = prologue | (in_seg & has_next & (k < MOE_CHUNKS))
    fetch_e = jnp.where(prologue, first_e,
                        jnp.where(in_seg, jnp.where(has_next, pick(next_e), e_s), last_e))
    fetch_c = jnp.where(prologue, s, jnp.where(in_seg & has_next,
                                               jnp.minimum(k, MOE_CHUNKS - 1), MOE_CHUNKS - 1))
    par_compute = pick(ordinal) % 2
    par_cast = jnp.where(prologue, 0, (pick(ordinal) + 1) % 2)
    flags = (compute.astype(i32) * MOE_COMPUTE + cast.astype(i32) * MOE_CAST
             + zero.astype(i32) * MOE_ZERO + (par_compute << MOE_PAR_COMPUTE_SHIFT)
             + (par_cast << MOE_PAR_CAST_SHIFT))
    return flags, rb, fetch_e, fetch_c, n_used.reshape(1)


def _row_tokens(meta, expert_ids, n_tok, nb):
    counts = meta[0, :N_EXPERTS].astype(jnp.int32)
    blocks = meta[1, :N_EXPERTS].astype(jnp.int32)
    n_slots = n_tok * TOP_K
    fill = blocks * MOE_TM - counts
    e_iota = jnp.arange(N_EXPERTS, dtype=jnp.int32)[:, None]
    r_iota = jnp.arange(MOE_TM, dtype=jnp.int32)[None, :]
    filler_keys = jnp.where(r_iota < fill[:, None], e_iota, N_EXPERTS).reshape(-1)
    order = jnp.argsort(jnp.concatenate([expert_ids.reshape(-1), filler_keys])).astype(jnp.int32)
    slot_tok = jnp.where(order < n_slots, order // TOP_K, 0)
    return (slot_tok * TOK_ROWS).reshape(nb, 1, MOE_TM)


def kernel(x, c, positions, w_ada, b_ada, w_in, conv_w, conv_b, conv_ln_g, conv_ln_b, w_out,
           ln1_g, ln1_b, w_group_router, b_group_router, w_expert_router, b_expert_router,
           w_gate, w_up, w_down, ln2_g, ln2_b):
    batch, seq, d = x.shape
    n_tok = batch * seq
    l = 0
    row = lambda a: a[l][None, :]

    c_pad = jnp.pad(c, ((0, SUBLANES - batch), (0, 0)))
    mod = _ada(c_pad, w_ada[l], b_ada[l][None, :])[:batch]
    shift1, scale1, gate1, shift2, scale2, gate2 = [m[:, None, :] for m in jnp.split(mod, 6, axis=-1)]

    x2 = x.reshape(n_tok, d)
    invf, sgn = _rope_tables()
    h0, q, k, v, gs = _inproj(x2, scale1, shift1, positions.reshape(n_tok, 1), invf, sgn,
                              w_in[l].astype(BF16), seq)

    hc = _conv(h0.reshape(batch, seq, D_CONV), conv_w[l], row(conv_b), row(conv_ln_g),
               row(conv_ln_b)).reshape(n_tok, D_CONV)

    dec, xi, zeta, cd = _retention_tables()
    r = _retention(q, k, v, gs, dec, xi, zeta, cd, batch, seq)

    n_route = N_GROUPS + N_EXPERTS
    w_router = jnp.pad(jnp.concatenate([w_group_router[l], w_expert_router[l]], axis=1),
                       ((0, 0), (0, LANES - n_route)))
    b_router = jnp.pad(jnp.concatenate([b_group_router[l], b_expert_router[l]]),
                       (0, LANES - n_route))[None, :]
    x1, u2, route, counts = _outproj(hc, r, x2, w_out[l].astype(BF16), gate1, row(ln1_g),
                                     row(ln1_b), scale2, shift2, w_router, b_router, seq)

    n_slots = n_tok * TOP_K
    nb = n_slots // MOE_TM + N_EXPERTS
    assert N_EXPERTS * (MOE_CHUNKS - 1) * MOE_TM < n_slots
    assert n_slots // MOE_TM >= MOE_XBUFS
    pos_pad, meta = _plan(route, counts)
    slot_tok3 = _row_tokens(meta, route[:, :TOP_K].astype(jnp.int32), n_tok, nb)
    y = _moe(_step_tables(meta, nb), slot_tok3, u2, w_gate[l], w_up[l], w_down[l])

    pos3 = (pos_pad[:, :TOP_K] * TOK_ROWS).reshape(n_tok // COMB_TM, 1, TOP_K * COMB_TM)
    out = _combine(pos3, y, x1, route, gate2, row(ln2_g), row(ln2_b), seq)
    return out.reshape(batch, seq, d)
```
